```python
import math
import jax, jax.numpy as jnp
from jax import lax
import numpy as np

D_MODEL = 1024
BATCH = 8
SEQ = 4096
DEPTH = 2

PLE_DIM = 256
HEAD_DIM = 128
MIX_WIDTH = D_MODEL
N_HEADS_A = MIX_WIDTH // (2 * HEAD_DIM)
A_QK_DIM = HEAD_DIM // 2
A_V_DIM = HEAD_DIM
N_HEADS_B = MIX_WIDTH // (2 * HEAD_DIM)
N_HEADS_C = MIX_WIDTH // HEAD_DIM
ROPE_THETA = 500000.0
ROPE_FRACTION = 4
QBLK = 128
DILATED_PATTERNS = ((128, 1), (512, 4), (2048, 16))
FORGET_BIAS_INIT = 2.0
RMS_EPS = 1e-6
N_EVEN = (DEPTH + 1) // 2
N_ODD = DEPTH // 2
A_QK_WIDTH = N_HEADS_A * 2 * A_QK_DIM
A_V_WIDTH = N_HEADS_A * A_V_DIM
B_WIDTH = N_HEADS_B * HEAD_DIM
EVEN_IN = 2 * A_QK_WIDTH + A_V_WIDTH + 3 * B_WIDTH + N_HEADS_B + MIX_WIDTH
ODD_IN = 4 * MIX_WIDTH

kernel_name = 'hybrid_diff_fox_dilated_block'


def rms_norm(x, g):
    x32 = x.astype(jnp.float32)
    y = x32 * lax.rsqrt(jnp.mean(x32 * x32, axis=-1, keepdims=True) + RMS_EPS)
    return (y * g.astype(jnp.float32)).astype(x.dtype)


def partial_rope(x, positions):
    rot = x.shape[-1] // ROPE_FRACTION
    half = rot // 2
    inv_freq = jnp.exp(-math.log(ROPE_THETA) * jnp.arange(half, dtype=jnp.float32) / half)
    ang = positions.astype(jnp.float32)[..., None] * inv_freq
    ang = ang.reshape(ang.shape[:2] + (1,) * (x.ndim - 3) + (half,))
    cos, sin = jnp.cos(ang), jnp.sin(ang)
    x1 = x[..., :half].astype(jnp.float32)
    x2 = x[..., half:rot].astype(jnp.float32)
    out = jnp.concatenate([x1 * cos - x2 * sin, x2 * cos + x1 * sin,
                           x[..., rot:].astype(jnp.float32)], axis=-1)
    return out.astype(x.dtype)


def to_qblocks(t):
    b, s = t.shape[:2]
    return jnp.moveaxis(t.reshape((b, s // QBLK, QBLK) + t.shape[2:]), 1, 0)


def from_qblocks(t):
    nb, b = t.shape[:2]
    return jnp.moveaxis(t, 0, 1).reshape((b, nb * QBLK) + t.shape[3:])


def diff_attention(q1, q2, k1, k2, v, lam):
    s_len = k1.shape[1]
    kpos = jnp.arange(s_len)

    def one_block(args):
        q1b, q2b, t0 = args
        causal = (t0 + jnp.arange(QBLK))[:, None] >= kpos[None, :]

        def probs(qb, k):
            s = jnp.einsum('bqhd,bkhd->bhqk', qb, k).astype(jnp.float32)
            return jax.nn.softmax(jnp.where(causal, s, -jnp.inf), axis=-1)

        w = probs(q1b, k1) - lam * probs(q2b, k2)
        return jnp.einsum('bhqk,bkhd->bqhd', w.astype(v.dtype), v)

    t0s = jnp.arange(s_len // QBLK, dtype=jnp.int32) * QBLK
    return from_qblocks(lax.map(one_block, (to_qblocks(q1), to_qblocks(q2), t0s)))


def fox_attention(q, k, v, cum_logf):
    s_len = k.shape[1]
    kpos = jnp.arange(s_len)
    c_k = jnp.swapaxes(cum_logf, 1, 2)

    def one_block(args):
        qb, cqb, t0 = args
        causal = (t0 + jnp.arange(QBLK))[:, None] >= kpos[None, :]
        s = jnp.einsum('bqhd,bkhd->bhqk', qb, k).astype(jnp.float32)
        s = s + jnp.swapaxes(cqb, 1, 2)[..., None] - c_k[:, :, None, :]
        pr = jax.nn.softmax(jnp.where(causal, s, -jnp.inf), axis=-1)
        return jnp.einsum('bhqk,bkhd->bqhd', pr.astype(v.dtype), v)

    t0s = jnp.arange(s_len // QBLK, dtype=jnp.int32) * QBLK
    return from_qblocks(lax.map(one_block, (to_qblocks(q), to_qblocks(cum_logf), t0s)))


def dilated_window_attention(q, k, v, window, dil):
    b, s_len, h, d = q.shape
    steps = window // dil
    blk = steps
    span = blk * dil
    s_pad = -(-s_len // span) * span
    pad = ((0, 0), (0, s_pad - s_len), (0, 0), (0, 0))
    q, k, v = jnp.pad(q, pad), jnp.pad(k, pad), jnp.pad(v, pad)
    nb = s_pad // span

    def split(t):
        return t.reshape(b, nb, blk, dil, h, d)

    def with_prev(t):
        prev = jnp.pad(t[:, :-1], ((0, 0), (1, 0), (0, 0), (0, 0), (0, 0), (0, 0)))
        return jnp.concatenate([prev, t], axis=2)

    qb = split(q)
    kk, vv = with_prev(split(k)), with_prev(split(v))
    s = jnp.einsum('bnqrhd,bnkrhd->bnrhqk', qb, kk).astype(jnp.float32)
    dist = jnp.arange(blk)[:, None] + blk - jnp.arange(2 * blk)[None, :]
    band = (dist >= 0) & (dist <= steps)
    exists = (jnp.arange(nb)[:, None, None] > 0) | (jnp.arange(2 * blk)[None, None, :] >= blk)
    valid = band[None] & exists
    s = jnp.where(valid[None, :, None, None], s, -jnp.inf)
    lse = jax.nn.logsumexp(s, axis=-1)
    pr = jnp.exp(s - lse[..., None])
    o = jnp.einsum('bnrhqk,bnkrhd->bnqrhd', pr.astype(v.dtype), vv)
    o = o.reshape(b, s_pad, h, d)[:, :s_len]
    lse = jnp.transpose(lse, (0, 1, 4, 2, 3)).reshape(b, s_pad, h)[:, :s_len]
    return o, lse


def even_layer(h, positions, norm_g, w_in, b_forget, qn_a, kn_a, qn_b, kn_b,
               lam_q1, lam_k1, lam_q2, lam_k2, subln_g, w_out, lam_init):
    b, s_len, _ = h.shape
    hn = rms_norm(h, norm_g)
    proj = hn @ w_in
    sizes = (A_QK_WIDTH, A_QK_WIDTH, A_V_WIDTH, B_WIDTH, B_WIDTH, B_WIDTH, N_HEADS_B)
    cuts, acc = [], 0
    for sz in sizes:
        acc += sz
        cuts.append(acc)
    qa, ka, va, qb, kb, vb, f_logit, z = jnp.split(proj, cuts, axis=-1)

    qa = partial_rope(rms_norm(qa.reshape(b, s_len, N_HEADS_A, 2, A_QK_DIM), qn_a), positions)
    qa = qa * (A_QK_DIM ** -0.5)
    ka = partial_rope(rms_norm(ka.reshape(b, s_len, N_HEADS_A, 2, A_QK_DIM), kn_a), positions)
    lam = (jnp.exp(jnp.sum(lam_q1.astype(jnp.float32) * lam_k1.astype(jnp.float32)))
           - jnp.exp(jnp.sum(lam_q2.astype(jnp.float32) * lam_k2.astype(jnp.float32)))
           + lam_init)
    oa = diff_attention(qa[..., 0, :], qa[..., 1, :], ka[..., 0, :], ka[..., 1, :],
                        va.reshape(b, s_len, N_HEADS_A, A_V_DIM), lam)
    oa = rms_norm(oa, subln_g) * (1.0 - lam_init)

    qb = rms_norm(qb.reshape(b, s_len, N_HEADS_B, HEAD_DIM), qn_b) * (HEAD_DIM ** -0.5)
    kb = rms_norm(kb.reshape(b, s_len, N_HEADS_B, HEAD_DIM), kn_b)
    cum_logf = jnp.cumsum(jax.nn.log_sigmoid((f_logit + b_forget).astype(jnp.float32)), axis=1)
    ob = fox_attention(qb, kb, vb.reshape(b, s_len, N_HEADS_B, HEAD_DIM), cum_logf)

    mixed = jnp.concatenate([oa.reshape(b, s_len, A_V_WIDTH), ob.reshape(b, s_len, B_WIDTH)],
                            axis=-1) * jax.nn.silu(z)
    return h + mixed @ w_out


def odd_layer(h, positions, norm_g, w_in, qn_c, kn_c, w_out):
    b, s_len, _ = h.shape
    hn = rms_norm(h, norm_g)
    proj = hn @ w_in
    q, k, v, z = jnp.split(proj, [MIX_WIDTH, 2 * MIX_WIDTH, 3 * MIX_WIDTH], axis=-1)
    q = partial_rope(rms_norm(q.reshape(b, s_len, N_HEADS_C, HEAD_DIM), qn_c), positions)
    q = q * (HEAD_DIM ** -0.5)
    k = partial_rope(rms_norm(k.reshape(b, s_len, N_HEADS_C, HEAD_DIM), kn_c), positions)
    v = v.reshape(b, s_len, N_HEADS_C, HEAD_DIM)
    outs, lses = [], []
    for window, dil in DILATED_PATTERNS:
        o_i, lse_i = dilated_window_attention(q, k, v, window, dil)
        outs.append(o_i)
        lses.append(lse_i)
    wts = jax.nn.softmax(jnp.stack(lses), axis=0)
    o = jnp.sum(wts[..., None] * jnp.stack(outs).astype(jnp.float32), axis=0).astype(h.dtype)
    mixed = o.reshape(b, s_len, MIX_WIDTH) * jax.nn.silu(z)
    return h + mixed @ w_out


def per_layer_embedding(h, p_i, w_ple, w_gate):
    return h + (p_i @ w_ple) * jax.nn.sigmoid(h @ w_gate)


def setup_inputs(seed: int = 0) -> dict:
    key = jax.random.key(seed)
    ks = jax.random.split(key, 24)

    def nrm(k, shape, scale):
        return scale * jax.random.normal(k, shape, jnp.float32)

    def gain(k, shape):
        return 1.0 + 0.02 * jax.random.normal(k, shape, jnp.float32)

    positions = (jnp.arange(SEQ, dtype=jnp.int32)[None, :]
                 + jax.random.randint(ks[2], (BATCH, 1), 0, 1024, dtype=jnp.int32))
    return {
        'x': nrm(ks[0], (BATCH, SEQ, D_MODEL), 1.0),
        'p': nrm(ks[1], (DEPTH, BATCH, SEQ, PLE_DIM), 1.0),
        'positions': positions,
        'norm_g': gain(ks[3], (DEPTH, D_MODEL)),
        'w_in_even': nrm(ks[4], (N_EVEN, D_MODEL, EVEN_IN), D_MODEL ** -0.5),
        'b_forget': FORGET_BIAS_INIT + 0.1 * jax.random.normal(ks[5], (N_EVEN, N_HEADS_B), jnp.float32),
        'qn_a': gain(ks[6], (N_EVEN, A_QK_DIM)),
        'kn_a': gain(ks[7], (N_EVEN, A_QK_DIM)),
        'qn_b': gain(ks[8], (N_EVEN, HEAD_DIM)),
        'kn_b': gain(ks[9], (N_EVEN, HEAD_DIM)),
        'lam_q1': nrm(ks[10], (N_EVEN, A_QK_DIM), 0.1),
        'lam_k1': nrm(ks[11], (N_EVEN, A_QK_DIM), 0.1),
        'lam_q2': nrm(ks[12], (N_EVEN, A_QK_DIM), 0.1),
        'lam_k2': nrm(ks[13], (N_EVEN, A_QK_DIM), 0.1),
        'subln_g': gain(ks[14], (N_EVEN, A_V_DIM)),
        'w_out_even': nrm(ks[15], (N_EVEN, MIX_WIDTH, D_MODEL), MIX_WIDTH ** -0.5),
        'w_in_odd': nrm(ks[16], (N_ODD, D_MODEL, ODD_IN), D_MODEL ** -0.5),
        'qn_c': gain(ks[17], (N_ODD, HEAD_DIM)),
        'kn_c': gain(ks[18], (N_ODD, HEAD_DIM)),
        'w_out_odd': nrm(ks[19], (N_ODD, MIX_WIDTH, D_MODEL), MIX_WIDTH ** -0.5),
        'w_ple': nrm(ks[20], (DEPTH, PLE_DIM, D_MODEL), PLE_DIM ** -0.5),
        'w_ple_gate': nrm(ks[21], (DEPTH, D_MODEL, D_MODEL), D_MODEL ** -0.5),
    }


def reference(x, p, positions, norm_g, w_in_even, b_forget, qn_a, kn_a, qn_b, kn_b,
              lam_q1, lam_k1, lam_q2, lam_k2, subln_g, w_out_even,
              w_in_odd, qn_c, kn_c, w_out_odd, w_ple, w_ple_gate):
    h = x
    for i in range(DEPTH):
        if i % 2 == 0:
            e = i // 2
            lam_init = 0.8 - 0.6 * math.exp(-0.3 * i)
            h = even_layer(h, positions, norm_g[i], w_in_even[e], b_forget[e], qn_a[e], kn_a[e],
                           qn_b[e], kn_b[e], lam_q1[e], lam_k1[e], lam_q2[e], lam_k2[e],
                           subln_g[e], w_out_even[e], lam_init)
        else:
            o = i // 2
            h = odd_layer(h, positions, norm_g[i], w_in_odd[o], qn_c[o], kn_c[o], w_out_odd[o])
        h = per_layer_embedding(h, p[i], w_ple[i], w_ple_gate[i])
    return h
```

```python
import functools
import math

import numpy as np
import jax
import jax.numpy as jnp
from jax import lax
from jax.experimental import pallas as pl
from jax.experimental.pallas import tpu as pltpu

D_MODEL = 1024
BATCH = 8
SEQ = 4096
PLE_DIM = 256
HEAD_DIM = 128
ROPE_THETA = 500000.0
RMS_EPS = 1e-6
LANES = 128
CHUNK = 512
N_RES = 16
SLAB = SEQ // N_RES
VMEM_LIMIT = 56 * 1024 * 1024
NEG = -1e30

bf16 = jnp.bfloat16
f32 = jnp.float32


def _nt_dot(a, b):
    return lax.dot_general(a, b, (((1,), (1,)), ((), ())), preferred_element_type=f32)


def _sigmoid(x):
    return 1.0 / (1.0 + jnp.exp(-x))


def _proj_kernel(x_ref, pos_ref, ng_ref, w_ref, gains_ref, ones_a_ref, ones_b_ref, lanec_ref,
                 *rest, chunks, rope_half, has_f):
    if has_f:
        wf_ref, bf_ref = rest[0], rest[1]
        out_refs = rest[2:]
    else:
        out_refs = rest
    x = x_ref[...].reshape(x_ref.shape[-2], x_ref.shape[-1])
    ms = jnp.mean(x * x, axis=-1, keepdims=True)
    hn = (x * lax.rsqrt(ms + RMS_EPS) * ng_ref[...]).astype(bf16)

    pos = pos_ref[...].reshape(x.shape[0], 1)
    ang = pos * lanec_ref[0:1, :]
    cosv = jnp.cos(ang)
    sinv = jnp.sin(ang)
    sin_hi = sinv * lanec_ref[1:2, :]
    sin_lo = sinv * lanec_ref[2:3, :]

    for c, (kind, group, oi, slot) in enumerate(chunks):
        acc = jnp.dot(hn, w_ref[:, c * CHUNK:(c + 1) * CHUNK], preferred_element_type=f32)
        if kind in ("rope", "norm"):
            ones_ref = ones_a_ref if group == 64 else ones_b_ref
            ss = jnp.dot((acc * acc).astype(bf16), ones_ref[...], preferred_element_type=f32)
            acc = acc * lax.rsqrt(ss * (1.0 / group) + RMS_EPS) * gains_ref[c:c + 1, :]
        o_ref = out_refs[oi]
        for hh in range(CHUNK // LANES):
            yb = acc[:, hh * LANES:(hh + 1) * LANES]
            if kind == "rope":
                yb = (yb * cosv + pltpu.roll(yb, rope_half, 1) * sin_hi
                      + pltpu.roll(yb, LANES - rope_half, 1) * sin_lo)
            yb = yb.astype(bf16)
            if kind == "z":
                col = slot * CHUNK + hh * LANES
                if len(o_ref.shape) == 3:
                    o_ref[0, :, col:col + LANES] = yb
                else:
                    o_ref[:, col:col + LANES] = yb
            else:
                head = slot * (CHUNK // LANES) + hh
                if len(o_ref.shape) == 5:
                    o_ref[0, head, 0] = yb
                else:
                    o_ref[0, head] = yb

    if has_f:
        f = jnp.dot(hn, wf_ref[...], preferred_element_type=f32) + bf_ref[...]
        out_refs[-1][...] = jnp.minimum(f, 0.0) - jnp.log1p(jnp.exp(-jnp.abs(f)))


def _lane_consts(group, half):
    lane = np.arange(LANES) % group
    inv_freq = jnp.exp(-math.log(ROPE_THETA) * jnp.arange(half, dtype=f32) / half)
    invf = jnp.where(lane < 2 * half, inv_freq[lane % half], 0.0)
    hi = ((lane >= half) & (lane < 2 * half)).astype(np.float32)
    lo = -(lane < half).astype(np.float32)
    rows = jnp.stack([invf, jnp.asarray(hi), jnp.asarray(lo)])
    return jnp.concatenate([rows, jnp.zeros((5, LANES), f32)], axis=0)


def _block_ones(group):
    i = np.arange(CHUNK)
    return jnp.asarray((i[:, None] // group == i[None, :] // group).astype(np.float32), dtype=bf16)


def _const_spec(shape):
    nd = len(shape)
    return pl.BlockSpec(shape, lambda *_: (0,) * nd)


def _proj_even(x2d, pos_col, ng, w_main, gains, w_f, b_f, tm):
    n_t = (BATCH * SEQ) // tm
    per_b = SEQ // tm
    chunks = (("rope", 64, 0, 0), ("rope", 64, 1, 0), ("plain", 0, 2, 0),
              ("norm", 128, 3, 0), ("norm", 128, 4, 0), ("plain", 0, 5, 0),
              ("z", 0, 6, 0), ("z", 0, 6, 1))
    head_shape = jax.ShapeDtypeStruct((BATCH, 4, SEQ, LANES), bf16)
    head_spec = pl.BlockSpec((1, 4, tm, LANES), lambda i: (i // per_b, 0, i % per_b, 0))
    row_spec = lambda w: pl.BlockSpec((tm, w), lambda i: (i, 0))
    kern = functools.partial(_proj_kernel, chunks=chunks, rope_half=8, has_f=True)
    return pl.pallas_call(
        kern,
        grid=(n_t,),
        in_specs=[row_spec(D_MODEL), row_spec(1), _const_spec((1, D_MODEL)),
                  _const_spec(w_main.shape), _const_spec((8, CHUNK)),
                  _const_spec((CHUNK, CHUNK)), _const_spec((CHUNK, CHUNK)),
                  _const_spec((8, LANES)), _const_spec((D_MODEL, LANES)), _const_spec((1, LANES))],
        out_specs=[head_spec] * 6 + [row_spec(D_MODEL), row_spec(LANES)],
        out_shape=[head_shape] * 6 + [jax.ShapeDtypeStruct((BATCH * SEQ, D_MODEL), bf16),
                                      jax.ShapeDtypeStruct((BATCH * SEQ, LANES), f32)],
        compiler_params=pltpu.CompilerParams(dimension_semantics=("arbitrary",),
                                             vmem_limit_bytes=VMEM_LIMIT),
        name="proj_even",
    )(x2d, pos_col, ng, w_main, gains, _block_ones(64), _block_ones(128), _lane_consts(64, 8),
      w_f, b_f)


def _proj_odd(h_res, pos_res, ng, w_main, gains):
    chunks = (("rope", 128, 0, 0), ("rope", 128, 0, 1), ("rope", 128, 1, 0), ("rope", 128, 1, 1),
              ("plain", 0, 2, 0), ("plain", 0, 2, 1), ("z", 0, 3, 0), ("z", 0, 3, 1))
    head_shape = jax.ShapeDtypeStruct((BATCH, 8, N_RES, SLAB, LANES), bf16)
    head_spec = pl.BlockSpec((1, 8, 1, SLAB, LANES), lambda b, r: (b, 0, r, 0, 0))
    kern = functools.partial(_proj_kernel, chunks=chunks, rope_half=16, has_f=False)
    return pl.pallas_call(
        kern,
        grid=(BATCH, N_RES),
        in_specs=[pl.BlockSpec((1, SLAB, D_MODEL), lambda b, r: (b, 0, r)),
                  pl.BlockSpec((1, 1, SLAB, 1), lambda b, r: (b, r, 0, 0)),
                  _const_spec((1, D_MODEL)), _const_spec(w_main.shape), _const_spec((8, CHUNK)),
                  _const_spec((CHUNK, CHUNK)), _const_spec((CHUNK, CHUNK)),
                  _const_spec((8, LANES))],
        out_specs=[head_spec] * 3 + [pl.BlockSpec((1, SLAB, D_MODEL), lambda b, r: (b * N_RES + r, 0, 0))],
        out_shape=[head_shape] * 3 + [jax.ShapeDtypeStruct((BATCH * N_RES, SLAB, D_MODEL), bf16)],
        compiler_params=pltpu.CompilerParams(dimension_semantics=("arbitrary", "arbitrary"),
                                             vmem_limit_bytes=VMEM_LIMIT),
        name="proj_odd",
    )(h_res, pos_res, ng, w_main, gains, _block_ones(64), _block_ones(128), _lane_consts(128, 16))


def _cumsum_kernel(logf_ref, c_ref, *, blk):
    row = lax.broadcasted_iota(jnp.int32, (blk, blk), 0)
    col = lax.broadcasted_iota(jnp.int32, (blk, blk), 1)
    upper = (row <= col).astype(bf16)
    carry = jnp.zeros((8, 1), f32)
    for i in range(SEQ // blk):
        xt = logf_ref[i * blk:(i + 1) * blk, :].T[0:8, :]
        hi = xt.astype(bf16)
        r1 = xt - hi.astype(f32)
        mid = r1.astype(bf16)
        lo = (r1 - mid.astype(f32)).astype(bf16)
        cs = (jnp.dot(hi, upper, preferred_element_type=f32)
              + jnp.dot(mid, upper, preferred_element_type=f32)
              + jnp.dot(lo, upper, preferred_element_type=f32)) + carry
        carry = cs[:, blk - 1:blk]
        for hh in range(4):
            c_ref[0, hh, :, i * blk:(i + 1) * blk] = cs[hh:hh + 1, :]


def _forget_cumsum(logf):
    return pl.pallas_call(
        functools.partial(_cumsum_kernel, blk=512),
        grid=(BATCH,),
        in_specs=[pl.BlockSpec((SEQ, LANES), lambda b: (b, 0))],
        out_specs=pl.BlockSpec((1, 4, 1, SEQ), lambda b: (b, 0, 0, 0)),
        out_shape=jax.ShapeDtypeStruct((BATCH, 4, 1, SEQ), f32),
        compiler_params=pltpu.CompilerParams(dimension_semantics=("arbitrary",)),
        name="forget_cumsum",
    )(logf)


def _flash_steps(qq, k_ref, v_ref, acc_sc, m_sc, l_sc, i, t, n_stack, key_bias):
    m_sc[...] = jnp.full(m_sc.shape, -jnp.inf, f32)
    l_sc[...] = jnp.zeros(l_sc.shape, f32)
    acc_sc[...] = jnp.zeros(acc_sc.shape, f32)

    def step(j, diag):
        k = k_ref[0, 0, pl.ds(j * t, t), :]
        v = v_ref[0, 0, pl.ds(j * t, t), :]
        s = _nt_dot(qq, k)
        if key_bias is not None:
            s = s - key_bias(j)
        if diag:
            rows = lax.broadcasted_iota(jnp.int32, s.shape, 0)
            cols = lax.broadcasted_iota(jnp.int32, s.shape, 1)
            if n_stack == 2:
                rows = jnp.where(rows >= t, rows - t, rows)
            s = jnp.where(rows >= cols, s, -jnp.inf)
        m_prev = m_sc[...]
        m_new = jnp.maximum(m_prev, jnp.max(s, axis=1, keepdims=True))
        alpha = jnp.exp(m_prev - m_new)
        p = jnp.exp(s - m_new)
        l_sc[...] = alpha * l_sc[...] + jnp.sum(p, axis=1, keepdims=True)
        acc_sc[...] = alpha * acc_sc[...] + jnp.dot(p.astype(bf16), v, preferred_element_type=f32)
        m_sc[...] = m_new

    def body(j, carry):
        step(j, False)
        return carry

    lax.fori_loop(0, i, body, 0)
    step(i, True)


def _diff_attn_kernel(q_ref, k_ref, v_ref, lam_ref, g_ref, o_ref, acc_sc, m_sc, l_sc, *, t, lam_init):
    i = pl.program_id(2)
    q = q_ref[0, 0]
    lane = lax.broadcasted_iota(jnp.int32, q.shape, 1)
    zero = jnp.zeros_like(q)
    qq = jnp.concatenate([jnp.where(lane < 64, q, zero), jnp.where(lane >= 64, q, zero)], axis=0)
    _flash_steps(qq, k_ref, v_ref, acc_sc, m_sc, l_sc, i, t, 2, None)
    lam = (jnp.exp(jnp.sum(lam_ref[0:1, :] * lam_ref[1:2, :], axis=1, keepdims=True))
           - jnp.exp(jnp.sum(lam_ref[2:3, :] * lam_ref[3:4, :], axis=1, keepdims=True)) + lam_init)
    o1 = acc_sc[0:t, :] / l_sc[0:t, :]
    o2 = acc_sc[t:2 * t, :] / l_sc[t:2 * t, :]
    o = o1 - lam * o2
    ms = jnp.mean(o * o, axis=-1, keepdims=True)
    o = o * lax.rsqrt(ms + RMS_EPS) * g_ref[...] * (1.0 - lam_init)
    o_ref[0, 0] = o.astype(bf16)


def _fox_attn_kernel(q_ref, k_ref, v_ref, c_ref, o_ref, acc_sc, m_sc, l_sc, *, t):
    i = pl.program_id(2)
    key_bias = lambda j: c_ref[0, 0, :, pl.ds(j * t, t)]
    _flash_steps(q_ref[0, 0], k_ref, v_ref, acc_sc, m_sc, l_sc, i, t, 1, key_bias)
    o_ref[0, 0] = (acc_sc[...] / l_sc[...]).astype(bf16)


def _layer0_attention(q, k, v, extra, extra_specs, kern, n_stack, t):
    nq = SEQ // t
    q_spec = pl.BlockSpec((1, 1, t, LANES), lambda b, h, i: (b, h, i, 0))
    kv_spec = pl.BlockSpec((1, 1, SEQ, LANES), lambda b, h, i: (b, h, 0, 0))
    return pl.pallas_call(
        kern,
        grid=(BATCH, 4, nq),
        in_specs=[q_spec, kv_spec, kv_spec] + extra_specs,
        out_specs=q_spec,
        out_shape=jax.ShapeDtypeStruct((BATCH, 4, SEQ, LANES), bf16),
        scratch_shapes=[pltpu.VMEM((n_stack * t, LANES), f32), pltpu.VMEM((n_stack * t, 1), f32),
                        pltpu.VMEM((n_stack * t, 1), f32)],
        compiler_params=pltpu.CompilerParams(
            dimension_semantics=("arbitrary", "arbitrary", "arbitrary"), vmem_limit_bytes=VMEM_LIMIT),
        name="diff_attn" if n_stack == 2 else "fox_attn",
    )(q, k, v, *extra)


def _pattern_bias(dil):
    n_c = N_RES // dil
    bq = max(128 // n_c, 16)
    bk = 2 * bq
    cq, iq = np.divmod(np.arange(n_c * bq), bq)
    ck, jk = np.divmod(np.arange(n_c * bk), bk)
    out = []
    for shift in (0, bq):
        dist = n_c * (shift + iq[:, None] - jk[None, :]) + (cq[:, None] - ck[None, :])
        out.append(np.where((dist >= 0) & (dist <= 128), 0.0, NEG))
    return jnp.asarray(np.stack(out), dtype=f32), n_c, bq


def _dilated_kernel(q_ref, k_ref, v_ref, b16_ref, b4_ref, b1_ref, o_ref, acc_sc, m_sc, l_sc, *, geo):
    def pieces(ref, slabs, start, n):
        parts = [ref[0, 0, s, pl.ds(start, n), :] for s in slabs]
        return parts[0] if len(parts) == 1 else jnp.concatenate(parts, axis=0)

    def sc_pieces(ref, slabs, start, n):
        parts = [ref[s, pl.ds(start, n), :] for s in slabs]
        return parts[0] if len(parts) == 1 else jnp.concatenate(parts, axis=0)

    def tile(slabs, ub, bq, bias_ref, first):
        u0 = ub * bq
        ks = jnp.maximum(ub - 1, 0) * bq
        q = pieces(q_ref, slabs, u0, bq)
        k = pieces(k_ref, slabs, ks, 2 * bq)
        v = pieces(v_ref, slabs, ks, 2 * bq)
        s = _nt_dot(q, k) + bias_ref[jnp.minimum(ub, 1)]
        m_cur = jnp.max(s, axis=1, keepdims=True)
        if first:
            m_new = m_cur
        else:
            m_prev = sc_pieces(m_sc, slabs, u0, bq)
            m_new = jnp.maximum(m_prev, m_cur)
            alpha = jnp.exp(m_prev - m_new)
        p = jnp.exp(s - m_new)
        l_new = jnp.sum(p, axis=1, keepdims=True)
        a_new = jnp.dot(p.astype(bf16), v, preferred_element_type=f32)
        if not first:
            l_new = l_new + alpha * sc_pieces(l_sc, slabs, u0, bq)
            a_new = a_new + alpha * sc_pieces(acc_sc, slabs, u0, bq)
        for n, s_id in enumerate(slabs):
            rows = slice(n * bq, (n + 1) * bq)
            m_sc[s_id, pl.ds(u0, bq), :] = m_new[rows]
            l_sc[s_id, pl.ds(u0, bq), :] = l_new[rows]
            acc_sc[s_id, pl.ds(u0, bq), :] = a_new[rows]

    first = True
    for (dil, n_c, bq), bias_ref in zip(geo, (b16_ref, b4_ref, b1_ref)):
        n_tiles = SLAB // bq
        if n_c == 1:
            def body(r, carry, bq=bq, bias_ref=bias_ref, first=first, n_tiles=n_tiles):
                for ub in range(n_tiles):
                    tile([r], ub, bq, bias_ref, first)
                return carry
            lax.fori_loop(0, N_RES, body, 0)
        else:
            for res in range(dil):
                slabs = [c * dil + res for c in range(n_c)]

                def body(ub, carry, slabs=slabs, bq=bq, bias_ref=bias_ref, first=first):
                    tile(slabs, ub, bq, bias_ref, first)
                    return carry
                lax.fori_loop(0, n_tiles, body, 0)
        first = False

    def fin(r, carry):
        o_ref[0, 0, r] = (acc_sc[r] / l_sc[r]).astype(bf16)
        return carry
    lax.fori_loop(0, N_RES, fin, 0)


def _dilated_attention(q, k, v):
    biases, geo = [], []
    for dil in (16, 4, 1):
        b, n_c, bq = _pattern_bias(dil)
        biases.append(b)
        geo.append((dil, n_c, bq))
    spec = pl.BlockSpec((1, 1, N_RES, SLAB, LANES), lambda b, h: (b, h, 0, 0, 0))
    return pl.pallas_call(
        functools.partial(_dilated_kernel, geo=tuple(geo)),
        grid=(BATCH, 8),
        in_specs=[spec, spec, spec] + [_const_spec(b.shape) for b in biases],
        out_specs=spec,
        out_shape=jax.ShapeDtypeStruct((BATCH, 8, N_RES, SLAB, LANES), bf16),
        scratch_shapes=[pltpu.VMEM((N_RES, SLAB, LANES), f32), pltpu.VMEM((N_RES, SLAB, 1), f32),
                        pltpu.VMEM((N_RES, SLAB, 1), f32)],
        compiler_params=pltpu.CompilerParams(dimension_semantics=("arbitrary", "arbitrary"),
                                             vmem_limit_bytes=VMEM_LIMIT),
        name="dilated_attn",
    )(q, k, v, *biases)


def _post_kernel(h_ref, *rest, n_o):
    o_refs = rest[:n_o]
    z_ref, p_ref, wo_ref, wg_ref, wp_ref, out_ref, mixed_sc, h1_sc = rest[n_o:]
    tm = mixed_sc.shape[0]
    heads_per = D_MODEL // LANES // n_o
    for n, o_ref in enumerate(o_refs):
        for hh in range(heads_per):
            col = (n * heads_per + hh) * LANES
            o = (o_ref[0, hh] if len(o_ref.shape) == 4 else o_ref[0, hh, 0]).astype(f32)
            z = (z_ref[0, :, col:col + LANES] if len(z_ref.shape) == 3
                 else z_ref[:, col:col + LANES]).astype(f32)
            mixed_sc[:, col:col + LANES] = (o * (z * _sigmoid(z))).astype(bf16)
    h = h_ref[...].reshape(tm, D_MODEL)
    h1 = h + jnp.dot(mixed_sc[...], wo_ref[...], preferred_element_type=f32)
    h1_sc[...] = h1
    h1b = h1.astype(bf16)
    pb = p_ref[...].reshape(tm, PLE_DIM).astype(bf16)
    for c in range(D_MODEL // CHUNK):
        cols = slice(c * CHUNK, (c + 1) * CHUNK)
        gate = _sigmoid(jnp.dot(h1b, wg_ref[:, cols], preferred_element_type=f32))
        emb = jnp.dot(pb, wp_ref[:, cols], preferred_element_type=f32)
        res = h1_sc[:, cols] + emb * gate
        if len(out_ref.shape) == 3:
            out_ref[0, :, cols] = res
        else:
            out_ref[:, cols] = res


def _post_even(x2d, oa, ob, z, p0, wo, wg, wp, tm):
    n_t = (BATCH * SEQ) // tm
    per_b = SEQ // tm
    row_spec = lambda w: pl.BlockSpec((tm, w), lambda i: (i, 0))
    head_spec = pl.BlockSpec((1, 4, tm, LANES), lambda i: (i // per_b, 0, i % per_b, 0))
    return pl.pallas_call(
        functools.partial(_post_kernel, n_o=2),
        grid=(n_t,),
        in_specs=[row_spec(D_MODEL), head_spec, head_spec, row_spec(D_MODEL), row_spec(PLE_DIM),
                  _const_spec(wo.shape), _const_spec(wg.shape), _const_spec(wp.shape)],
        out_specs=row_spec(D_MODEL),
        out_shape=jax.ShapeDtypeStruct((BATCH * SEQ, D_MODEL), f32),
        scratch_shapes=[pltpu.VMEM((tm, D_MODEL), bf16), pltpu.VMEM((tm, D_MODEL), f32)],
        compiler_params=pltpu.CompilerParams(dimension_semantics=("arbitrary",),
                                             vmem_limit_bytes=VMEM_LIMIT),
        name="post_even",
    )(x2d, oa, ob, z, p0, wo, wg, wp)


def _post_odd(h_res, o, z, p_res, wo, wg, wp):
    res_row = lambda w: pl.BlockSpec((1, SLAB, w), lambda b, r: (b, 0, r))
    return pl.pallas_call(
        functools.partial(_post_kernel, n_o=1),
        grid=(BATCH, N_RES),
        in_specs=[res_row(D_MODEL),
                  pl.BlockSpec((1, 8, 1, SLAB, LANES), lambda b, r: (b, 0, r, 0, 0)),
                  pl.BlockSpec((1, SLAB, D_MODEL), lambda b, r: (b * N_RES + r, 0, 0)),
                  res_row(PLE_DIM),
                  _const_spec(wo.shape), _const_spec(wg.shape), _const_spec(wp.shape)],
        out_specs=res_row(D_MODEL),
        out_shape=jax.ShapeDtypeStruct((BATCH, SLAB, N_RES * D_MODEL), f32),
        scratch_shapes=[pltpu.VMEM((SLAB, D_MODEL), bf16), pltpu.VMEM((SLAB, D_MODEL), f32)],
        compiler_params=pltpu.CompilerParams(dimension_semantics=("arbitrary", "arbitrary"),
                                             vmem_limit_bytes=VMEM_LIMIT),
        name="post_odd",
    )(h_res, o, z, p_res, wo, wg, wp)


def _tile4(g):
    return jnp.tile(g.astype(f32), CHUNK // g.shape[0])


def kernel(x, p, positions, norm_g, w_in_even, b_forget, qn_a, kn_a, qn_b, kn_b, lam_q1, lam_k1,
           lam_q2, lam_k2, subln_g, w_out_even, w_in_odd, qn_c, kn_c, w_out_odd, w_ple, w_ple_gate):
    tm = 512
    t_attn = 256
    n_tok = BATCH * SEQ
    x2d = x.reshape(n_tok, D_MODEL)
    pos_f = positions.astype(f32)

    w0 = w_in_even[0]
    f_lo, f_hi = 3072, 3076
    w_main0 = jnp.concatenate([w0[:, :f_lo], w0[:, f_hi:]], axis=1).astype(bf16)
    w_f = jnp.pad(w0[:, f_lo:f_hi], ((0, 0), (0, LANES - 4))).astype(bf16)
    b_f = jnp.pad(b_forget[0].astype(f32), (0, LANES - 4)).reshape(1, LANES)
    ones_row = jnp.ones((CHUNK,), f32)
    gains0 = jnp.stack([_tile4(qn_a[0]) * (64 ** -0.5), _tile4(kn_a[0]), ones_row,
                        _tile4(qn_b[0]) * (HEAD_DIM ** -0.5), _tile4(kn_b[0]), ones_row,
                        ones_row, ones_row])
    qa, ka, va, qb, kb, vb, z0, logf = _proj_even(
        x2d, pos_f.reshape(n_tok, 1), norm_g[0].reshape(1, D_MODEL).astype(f32), w_main0, gains0,
        w_f, b_f, tm)
    cum = _forget_cumsum(logf)

    lam_rows = jnp.pad(jnp.stack([lam_q1[0], lam_k1[0], lam_q2[0], lam_k2[0]]).astype(f32),
                       ((0, 4), (0, LANES - 64)))
    lam_init = 0.8 - 0.6 * math.exp(-0.3 * 0)
    oa = _layer0_attention(
        qa, ka, va, [lam_rows, subln_g[0].reshape(1, LANES).astype(f32)],
        [_const_spec((8, LANES)), _const_spec((1, LANES))],
        functools.partial(_diff_attn_kernel, t=t_attn, lam_init=lam_init), 2, t_attn)
    ob = _layer0_attention(
        qb, kb, vb, [cum], [pl.BlockSpec((1, 1, 1, SEQ), lambda b, h, i: (b, h, 0, 0))],
        functools.partial(_fox_attn_kernel, t=t_attn), 1, t_attn)
    h1 = _post_even(x2d, oa, ob, z0, p[0].reshape(n_tok, PLE_DIM), w_out_even[0].astype(bf16),
                    w_ple_gate[0].astype(bf16), w_ple[0].astype(bf16), tm)

    h_res = h1.reshape(BATCH, SLAB, N_RES * D_MODEL)
    pos_res = pos_f.reshape(BATCH, SLAB, N_RES).transpose(0, 2, 1).reshape(BATCH, N_RES, SLAB, 1)
    ones_row2 = jnp.ones((CHUNK,), f32)
    gq = _tile4(qn_c[0]) * (HEAD_DIM ** -0.5)
    gk = _tile4(kn_c[0])
    gains1 = jnp.stack([gq, gq, gk, gk, ones_row2, ones_row2, ones_row2, ones_row2])
    q1, k1, v1, z1 = _proj_odd(h_res, pos_res, norm_g[1].reshape(1, D_MODEL).astype(f32),
                               w_in_odd[0].astype(bf16), gains1)
    o1 = _dilated_attention(q1, k1, v1)
    out = _post_odd(h_res, o1, z1, p[1].reshape(BATCH, SLAB, N_RES * PLE_DIM),
                    w_out_odd[0].astype(bf16), w_ple_gate[1].astype(bf16), w_ple[1].astype(bf16))
    return out.reshape(BATCH, SEQ, D_MODEL)
```

```python
import functools
import math

import numpy as np
import jax
import jax.numpy as jnp
from jax import lax
from jax.experimental import pallas as pl
from jax.experimental.pallas import tpu as pltpu

D_MODEL = 1024
BATCH = 8
SEQ = 4096
PLE_DIM = 256
HEAD_DIM = 128
ROPE_THETA = 500000.0
RMS_EPS = 1e-6
LANES = 128
CHUNK = 512
N_RES = 16
SLAB = SEQ // N_RES
VMEM_LIMIT = 56 * 1024 * 1024
NEG = -1e30
LOG2E = math.log2(math.e)

bf16 = jnp.bfloat16
f32 = jnp.float32


def _nt_dot(a, b):
    return lax.dot_general(a, b, (((1,), (1,)), ((), ())), preferred_element_type=f32)


def _sigmoid(x):
    return 1.0 / (1.0 + jnp.exp(-x))


def _proj_kernel(x_ref, pos_ref, ng_ref, w_ref, gains_ref, ones_a_ref, ones_b_ref, lanec_ref,
                 *rest, chunks, rope_half, has_f):
    if has_f:
        wf_ref, bf_ref = rest[0], rest[1]
        out_refs = rest[2:]
    else:
        out_refs = rest
    x = x_ref[...].reshape(x_ref.shape[-2], x_ref.shape[-1])
    ms = jnp.mean(x * x, axis=-1, keepdims=True)
    hn = (x * lax.rsqrt(ms + RMS_EPS) * ng_ref[...]).astype(bf16)

    pos = pos_ref[...].reshape(x.shape[0], 1)
    ang = pos * lanec_ref[0:1, :]
    cosv = jnp.cos(ang)
    sinv = jnp.sin(ang)
    sin_hi = sinv * lanec_ref[1:2, :]
    sin_lo = sinv * lanec_ref[2:3, :]

    for c, (kind, group, oi, slot) in enumerate(chunks):
        acc = jnp.dot(hn, w_ref[:, c * CHUNK:(c + 1) * CHUNK], preferred_element_type=f32)
        if kind in ("rope", "norm"):
            ones_ref = ones_a_ref if group == 64 else ones_b_ref
            ss = jnp.dot((acc * acc).astype(bf16), ones_ref[...], preferred_element_type=f32)
            acc = acc * lax.rsqrt(ss * (1.0 / group) + RMS_EPS) * gains_ref[c:c + 1, :]
        o_ref = out_refs[oi]
        for hh in range(CHUNK // LANES):
            yb = acc[:, hh * LANES:(hh + 1) * LANES]
            if kind == "rope":
                yb = (yb * cosv + pltpu.roll(yb, rope_half, 1) * sin_hi
                      + pltpu.roll(yb, LANES - rope_half, 1) * sin_lo)
            if kind == "plain_t":
                o_ref[0, slot * (CHUNK // LANES) + hh] = yb.T.astype(bf16)
                continue
            yb = yb.astype(bf16)
            if kind == "z":
                col = slot * CHUNK + hh * LANES
                if len(o_ref.shape) == 3:
                    o_ref[0, :, col:col + LANES] = yb
                else:
                    o_ref[:, col:col + LANES] = yb
            else:
                head = slot * (CHUNK // LANES) + hh
                if len(o_ref.shape) == 5:
                    o_ref[0, head, 0] = yb
                else:
                    o_ref[0, head] = yb

    if has_f:
        f = jnp.dot(hn, wf_ref[...], preferred_element_type=f32) + bf_ref[...]
        out_refs[-1][...] = jnp.minimum(f, 0.0) - jnp.log1p(jnp.exp(-jnp.abs(f)))


def _lane_consts(group, half):
    lane = np.arange(LANES) % group
    inv_freq = jnp.exp(-math.log(ROPE_THETA) * jnp.arange(half, dtype=f32) / half)
    invf = jnp.where(lane < 2 * half, inv_freq[lane % half], 0.0)
    hi = ((lane >= half) & (lane < 2 * half)).astype(np.float32)
    lo = -(lane < half).astype(np.float32)
    rows = jnp.stack([invf, jnp.asarray(hi), jnp.asarray(lo)])
    return jnp.concatenate([rows, jnp.zeros((5, LANES), f32)], axis=0)


def _block_ones(group):
    i = np.arange(CHUNK)
    return jnp.asarray((i[:, None] // group == i[None, :] // group).astype(np.float32), dtype=bf16)


def _const_spec(shape):
    nd = len(shape)
    return pl.BlockSpec(shape, lambda *_: (0,) * nd)


def _proj_even(x2d, pos_col, ng, w_main, gains, w_f, b_f, tm):
    n_t = (BATCH * SEQ) // tm
    per_b = SEQ // tm
    chunks = (("rope", 64, 0, 0), ("rope", 64, 1, 0), ("plain_t", 0, 2, 0),
              ("norm", 128, 3, 0), ("norm", 128, 4, 0), ("plain_t", 0, 5, 0),
              ("z", 0, 6, 0), ("z", 0, 6, 1))
    head_shape = jax.ShapeDtypeStruct((BATCH, 4, SEQ, LANES), bf16)
    head_spec = pl.BlockSpec((1, 4, tm, LANES), lambda i: (i // per_b, 0, i % per_b, 0))
    head_t_shape = jax.ShapeDtypeStruct((BATCH, 4, LANES, SEQ), bf16)
    head_t_spec = pl.BlockSpec((1, 4, LANES, tm), lambda i: (i // per_b, 0, 0, i % per_b))
    row_spec = lambda w: pl.BlockSpec((tm, w), lambda i: (i, 0))
    kern = functools.partial(_proj_kernel, chunks=chunks, rope_half=8, has_f=True)
    return pl.pallas_call(
        kern,
        grid=(n_t,),
        in_specs=[row_spec(D_MODEL), row_spec(1), _const_spec((1, D_MODEL)),
                  _const_spec(w_main.shape), _const_spec((8, CHUNK)),
                  _const_spec((CHUNK, CHUNK)), _const_spec((CHUNK, CHUNK)),
                  _const_spec((8, LANES)), _const_spec((D_MODEL, LANES)), _const_spec((1, LANES))],
        out_specs=[head_spec, head_spec, head_t_spec] * 2 + [row_spec(D_MODEL), row_spec(LANES)],
        out_shape=[head_shape, head_shape, head_t_shape] * 2 + [
            jax.ShapeDtypeStruct((BATCH * SEQ, D_MODEL), bf16),
            jax.ShapeDtypeStruct((BATCH * SEQ, LANES), f32)],
        compiler_params=pltpu.CompilerParams(dimension_semantics=("arbitrary",),
                                             vmem_limit_bytes=VMEM_LIMIT),
        name="proj_even",
    )(x2d, pos_col, ng, w_main, gains, _block_ones(64), _block_ones(128), _lane_consts(64, 8),
      w_f, b_f)


def _proj_odd(h_res, pos_res, ng, w_main, gains):
    chunks = (("rope", 128, 0, 0), ("rope", 128, 0, 1), ("rope", 128, 1, 0), ("rope", 128, 1, 1),
              ("plain", 0, 2, 0), ("plain", 0, 2, 1), ("z", 0, 3, 0), ("z", 0, 3, 1))
    head_shape = jax.ShapeDtypeStruct((BATCH, 8, N_RES, SLAB, LANES), bf16)
    head_spec = pl.BlockSpec((1, 8, 1, SLAB, LANES), lambda b, r: (b, 0, r, 0, 0))
    kern = functools.partial(_proj_kernel, chunks=chunks, rope_half=16, has_f=False)
    return pl.pallas_call(
        kern,
        grid=(BATCH, N_RES),
        in_specs=[pl.BlockSpec((1, SLAB, D_MODEL), lambda b, r: (b, 0, r)),
                  pl.BlockSpec((1, 1, SLAB, 1), lambda b, r: (b, r, 0, 0)),
                  _const_spec((1, D_MODEL)), _const_spec(w_main.shape), _const_spec((8, CHUNK)),
                  _const_spec((CHUNK, CHUNK)), _const_spec((CHUNK, CHUNK)),
                  _const_spec((8, LANES))],
        out_specs=[head_spec] * 3 + [pl.BlockSpec((1, SLAB, D_MODEL), lambda b, r: (b * N_RES + r, 0, 0))],
        out_shape=[head_shape] * 3 + [jax.ShapeDtypeStruct((BATCH * N_RES, SLAB, D_MODEL), bf16)],
        compiler_params=pltpu.CompilerParams(dimension_semantics=("arbitrary", "arbitrary"),
                                             vmem_limit_bytes=VMEM_LIMIT),
        name="proj_odd",
    )(h_res, pos_res, ng, w_main, gains, _block_ones(64), _block_ones(128), _lane_consts(128, 16))


def _split3(x):
    hi = x.astype(bf16).astype(f32)
    r1 = x - hi
    mid = r1.astype(bf16).astype(f32)
    lo = (r1 - mid).astype(bf16).astype(f32)
    return hi, mid, lo


def _lane_place(lane, base, terms, init):
    out = init
    for n, t in enumerate(terms):
        out = jnp.where(lane == base + n, t, out)
    return out


def _cumsum_kernel(logf_ref, aug_ref, *, blk):
    row = lax.broadcasted_iota(jnp.int32, (blk, blk), 0)
    col = lax.broadcasted_iota(jnp.int32, (blk, blk), 1)
    lower = (col <= row).astype(bf16)
    lane = lax.broadcasted_iota(jnp.int32, (blk, LANES), 1)
    carry = jnp.zeros((1, LANES), f32)
    for i in range(SEQ // blk):
        x = logf_ref[i * blk:(i + 1) * blk, :]
        packed = jnp.zeros((blk, LANES), f32)
        for hh in range(4):
            xh = jnp.broadcast_to(x[:, hh:hh + 1], (blk, LANES))
            packed = _lane_place(lane, 3 * hh, _split3(xh), packed)
        cs = jnp.dot(lower, packed.astype(bf16), preferred_element_type=f32) + carry
        carry = cs[blk - 1:blk, :]
        for hh in range(4):
            c = (cs[:, 3 * hh:3 * hh + 1] + cs[:, 3 * hh + 1:3 * hh + 2]) + cs[:, 3 * hh + 2:3 * hh + 3]
            cb = jnp.broadcast_to(c * (-LOG2E), (blk, LANES))
            aug_ref[0, hh, i * blk:(i + 1) * blk, :] = _lane_place(
                lane, 0, _split3(cb), jnp.zeros((blk, LANES), f32)).astype(bf16)


def _forget_cumsum(logf):
    return pl.pallas_call(
        functools.partial(_cumsum_kernel, blk=512),
        grid=(BATCH,),
        in_specs=[pl.BlockSpec((SEQ, LANES), lambda b: (b, 0))],
        out_specs=pl.BlockSpec((1, 4, SEQ, LANES), lambda b: (b, 0, 0, 0)),
        out_shape=jax.ShapeDtypeStruct((BATCH, 4, SEQ, LANES), bf16),
        compiler_params=pltpu.CompilerParams(dimension_semantics=("arbitrary",),
                                             vmem_limit_bytes=VMEM_LIMIT),
        name="forget_cumsum",
    )(logf)


def _flash_sweep(qq, k_ref, vt_ref, aug_ref, acc_sc, m_sc, l_sc, i, t):
    n = qq.shape[0]
    m_sc[...] = jnp.full(m_sc.shape, -jnp.inf, f32)
    l_sc[...] = jnp.zeros(l_sc.shape, f32)
    acc_sc[...] = jnp.zeros(acc_sc.shape, f32)

    def scores(j):
        k = k_ref[0, 0, pl.ds(j * t, t), :]
        if aug_ref is not None:
            k = jnp.concatenate([k, aug_ref[0, 0, pl.ds(j * t, t), :]], axis=1)
        return _nt_dot(k, qq)

    def update(s, j, diag):
        if diag:
            key = lax.broadcasted_iota(jnp.int32, s.shape, 0)
            qry = lax.broadcasted_iota(jnp.int32, s.shape, 1)
            if n > t:
                qry = jnp.where(qry >= t, qry - t, qry)
            s = jnp.where(key <= qry, s, -jnp.inf)
        vt = vt_ref[0, 0, :, pl.ds(j * t, t)]
        m_prev = m_sc[...]
        m_new = jnp.maximum(m_prev, jnp.max(s, axis=0, keepdims=True))
        alpha = jnp.exp2(m_prev - m_new)
        p = jnp.exp2(s - m_new)
        l_sc[...] = alpha * l_sc[...] + jnp.sum(p, axis=0, keepdims=True)
        acc_sc[...] = alpha * acc_sc[...] + jnp.dot(vt, p.astype(bf16), preferred_element_type=f32)
        m_sc[...] = m_new

    def pair(j0, diag1):
        s0 = scores(j0)
        s1 = scores(j0 + 1)
        update(s0, j0, False)
        update(s1, j0 + 1, diag1)

    def body(pidx, carry):
        pair(2 * pidx, False)
        return carry

    lax.fori_loop(0, i // 2, body, 0)

    @pl.when(i % 2 == 1)
    def _():
        pair(i - 1, True)

    @pl.when(i % 2 == 0)
    def _():
        update(scores(i), i, True)


def _diff_attn_kernel(q_ref, k_ref, vt_ref, lam_ref, g_ref, o_ref, acc_sc, m_sc, l_sc, *,
                      tq, lam_init):
    i = pl.program_id(2)
    q = q_ref[0, 0]
    lane = lax.broadcasted_iota(jnp.int32, q.shape, 1)
    zero = jnp.zeros_like(q)
    qq = jnp.concatenate([jnp.where(lane < 64, q, zero), jnp.where(lane >= 64, q, zero)], axis=0)
    _flash_sweep(qq, k_ref, vt_ref, None, acc_sc, m_sc, l_sc, i, tq)
    lam = (jnp.exp(jnp.sum(lam_ref[0:1, :] * lam_ref[1:2, :], axis=1, keepdims=True))
           - jnp.exp(jnp.sum(lam_ref[2:3, :] * lam_ref[3:4, :], axis=1, keepdims=True)) + lam_init)
    ot = acc_sc[...] / l_sc[...]
    ot = ot[:, 0:tq] - lam * ot[:, tq:2 * tq]
    ms = jnp.mean(ot * ot, axis=0, keepdims=True)
    ot = ot * lax.rsqrt(ms + RMS_EPS) * g_ref[...] * (1.0 - lam_init)
    o_ref[0, 0] = ot.T.astype(bf16)


def _fox_attn_kernel(q_ref, k_ref, vt_ref, aug_ref, o_ref, acc_sc, m_sc, l_sc, *, tq):
    i = pl.program_id(2)
    q = q_ref[0, 0]
    lane = lax.broadcasted_iota(jnp.int32, q.shape, 1)
    ones3 = jnp.where(lane < 3, 1.0, 0.0).astype(bf16)
    qq = jnp.concatenate([q, ones3], axis=1)
    _flash_sweep(qq, k_ref, vt_ref, aug_ref, acc_sc, m_sc, l_sc, i, tq)
    o_ref[0, 0] = (acc_sc[...] / l_sc[...]).T.astype(bf16)


def _layer0_attention(q, k, vt, extra, extra_specs, kern, n_stack, tq, name):
    nq = SEQ // tq
    q_spec = pl.BlockSpec((1, 1, tq, LANES), lambda b, h, i: (b, h, i, 0))
    k_spec = pl.BlockSpec((1, 1, SEQ, LANES), lambda b, h, i: (b, h, 0, 0))
    vt_spec = pl.BlockSpec((1, 1, LANES, SEQ), lambda b, h, i: (b, h, 0, 0))
    n = n_stack * tq
    return pl.pallas_call(
        kern,
        grid=(BATCH, 4, nq),
        in_specs=[q_spec, k_spec, vt_spec] + extra_specs,
        out_specs=q_spec,
        out_shape=jax.ShapeDtypeStruct((BATCH, 4, SEQ, LANES), bf16),
        scratch_shapes=[pltpu.VMEM((LANES, n), f32), pltpu.VMEM((1, n), f32), pltpu.VMEM((1, n), f32)],
        compiler_params=pltpu.CompilerParams(
            dimension_semantics=("arbitrary", "arbitrary", "arbitrary"), vmem_limit_bytes=VMEM_LIMIT),
        name=name,
    )(q, k, vt, *extra)


def _pattern_bias(dil):
    n_c = N_RES // dil
    bq = max(128 // n_c, 16)
    bk = 2 * bq
    cq, iq = np.divmod(np.arange(n_c * bq), bq)
    ck, jk = np.divmod(np.arange(n_c * bk), bk)
    out = []
    for shift in (0, bq):
        dist = n_c * (shift + iq[:, None] - jk[None, :]) + (cq[:, None] - ck[None, :])
        out.append(np.where((dist >= 0) & (dist <= 128), 0.0, NEG))
    return jnp.asarray(np.stack(out), dtype=f32), n_c, bq


def _dilated_kernel(q_ref, k_ref, v_ref, b16_ref, b4_ref, b1_ref, o_ref, acc_sc, m_sc, l_sc, *, geo):
    def pieces(ref, slabs, start, n):
        parts = [ref[0, 0, s, pl.ds(start, n), :] for s in slabs]
        return parts[0] if len(parts) == 1 else jnp.concatenate(parts, axis=0)

    def sc_pieces(ref, slabs, start, n):
        parts = [ref[s, pl.ds(start, n), :] for s in slabs]
        return parts[0] if len(parts) == 1 else jnp.concatenate(parts, axis=0)

    def tile(slabs, ub, bq, bias_ref, first):
        u0 = ub * bq
        ks = jnp.maximum(ub - 1, 0) * bq
        q = pieces(q_ref, slabs, u0, bq)
        k = pieces(k_ref, slabs, ks, 2 * bq)
        v = pieces(v_ref, slabs, ks, 2 * bq)
        s = _nt_dot(q, k) + bias_ref[jnp.minimum(ub, 1)]
        m_cur = jnp.max(s, axis=1, keepdims=True)
        if first:
            m_new = m_cur
        else:
            m_prev = sc_pieces(m_sc, slabs, u0, bq)
            m_new = jnp.maximum(m_prev, m_cur)
            alpha = jnp.exp(m_prev - m_new)
        p = jnp.exp(s - m_new)
        l_new = jnp.sum(p, axis=1, keepdims=True)
        a_new = jnp.dot(p.astype(bf16), v, preferred_element_type=f32)
        if not first:
            l_new = l_new + alpha * sc_pieces(l_sc, slabs, u0, bq)
            a_new = a_new + alpha * sc_pieces(acc_sc, slabs, u0, bq)
        for n, s_id in enumerate(slabs):
            rows = slice(n * bq, (n + 1) * bq)
            m_sc[s_id, pl.ds(u0, bq), :] = m_new[rows]
            l_sc[s_id, pl.ds(u0, bq), :] = l_new[rows]
            acc_sc[s_id, pl.ds(u0, bq), :] = a_new[rows]

    first = True
    for (dil, n_c, bq), bias_ref in zip(geo, (b16_ref, b4_ref, b1_ref)):
        n_tiles = SLAB // bq
        if n_c == 1:
            def body(r, carry, bq=bq, bias_ref=bias_ref, first=first, n_tiles=n_tiles):
                for ub in range(n_tiles):
                    tile([r], ub, bq, bias_ref, first)
                return carry
            lax.fori_loop(0, N_RES, body, 0)
        else:
            for res in range(dil):
                slabs = [c * dil + res for c in range(n_c)]

                def body(ub, carry, slabs=slabs, bq=bq, bias_ref=bias_ref, first=first):
                    tile(slabs, ub, bq, bias_ref, first)
                    return carry
                lax.fori_loop(0, n_tiles, body, 0)
        first = False

    def fin(r, carry):
        o_ref[0, 0, r] = (acc_sc[r] / l_sc[r]).astype(bf16)
        return carry
    lax.fori_loop(0, N_RES, fin, 0)


def _dilated_attention(q, k, v):
    biases, geo = [], []
    for dil in (16, 4, 1):
        b, n_c, bq = _pattern_bias(dil)
        biases.append(b)
        geo.append((dil, n_c, bq))
    spec = pl.BlockSpec((1, 1, N_RES, SLAB, LANES), lambda b, h: (b, h, 0, 0, 0))
    return pl.pallas_call(
        functools.partial(_dilated_kernel, geo=tuple(geo)),
        grid=(BATCH, 8),
        in_specs=[spec, spec, spec] + [_const_spec(b.shape) for b in biases],
        out_specs=spec,
        out_shape=jax.ShapeDtypeStruct((BATCH, 8, N_RES, SLAB, LANES), bf16),
        scratch_shapes=[pltpu.VMEM((N_RES, SLAB, LANES), f32), pltpu.VMEM((N_RES, SLAB, 1), f32),
                        pltpu.VMEM((N_RES, SLAB, 1), f32)],
        compiler_params=pltpu.CompilerParams(dimension_semantics=("arbitrary", "arbitrary"),
                                             vmem_limit_bytes=VMEM_LIMIT),
        name="dilated_attn",
    )(q, k, v, *biases)


def _post_kernel(h_ref, *rest, n_o):
    o_refs = rest[:n_o]
    z_ref, p_ref, wo_ref, wg_ref, wp_ref, out_ref, mixed_sc, h1_sc = rest[n_o:]
    tm = mixed_sc.shape[0]
    heads_per = D_MODEL // LANES // n_o
    for n, o_ref in enumerate(o_refs):
        for hh in range(heads_per):
            col = (n * heads_per + hh) * LANES
            o = (o_ref[0, hh] if len(o_ref.shape) == 4 else o_ref[0, hh, 0]).astype(f32)
            z = (z_ref[0, :, col:col + LANES] if len(z_ref.shape) == 3
                 else z_ref[:, col:col + LANES]).astype(f32)
            mixed_sc[:, col:col + LANES] = (o * (z * _sigmoid(z))).astype(bf16)
    h = h_ref[...].reshape(tm, D_MODEL)
    h1 = h + jnp.dot(mixed_sc[...], wo_ref[...], preferred_element_type=f32)
    h1_sc[...] = h1
    h1b = h1.astype(bf16)
    pb = p_ref[...].reshape(tm, PLE_DIM).astype(bf16)
    for c in range(D_MODEL // CHUNK):
        cols = slice(c * CHUNK, (c + 1) * CHUNK)
        gate = _sigmoid(jnp.dot(h1b, wg_ref[:, cols], preferred_element_type=f32))
        emb = jnp.dot(pb, wp_ref[:, cols], preferred_element_type=f32)
        res = h1_sc[:, cols] + emb * gate
        if len(out_ref.shape) == 3:
            out_ref[0, :, cols] = res
        else:
            out_ref[:, cols] = res


def _post_even(x2d, oa, ob, z, p0, wo, wg, wp, tm):
    n_t = (BATCH * SEQ) // tm
    per_b = SEQ // tm
    row_spec = lambda w: pl.BlockSpec((tm, w), lambda i: (i, 0))
    head_spec = pl.BlockSpec((1, 4, tm, LANES), lambda i: (i // per_b, 0, i % per_b, 0))
    return pl.pallas_call(
        functools.partial(_post_kernel, n_o=2),
        grid=(n_t,),
        in_specs=[row_spec(D_MODEL), head_spec, head_spec, row_spec(D_MODEL), row_spec(PLE_DIM),
                  _const_spec(wo.shape), _const_spec(wg.shape), _const_spec(wp.shape)],
        out_specs=row_spec(D_MODEL),
        out_shape=jax.ShapeDtypeStruct((BATCH * SEQ, D_MODEL), f32),
        scratch_shapes=[pltpu.VMEM((tm, D_MODEL), bf16), pltpu.VMEM((tm, D_MODEL), f32)],
        compiler_params=pltpu.CompilerParams(dimension_semantics=("arbitrary",),
                                             vmem_limit_bytes=VMEM_LIMIT),
        name="post_even",
    )(x2d, oa, ob, z, p0, wo, wg, wp)


def _post_odd(h_res, o, z, p_res, wo, wg, wp):
    res_row = lambda w: pl.BlockSpec((1, SLAB, w), lambda b, r: (b, 0, r))
    return pl.pallas_call(
        functools.partial(_post_kernel, n_o=1),
        grid=(BATCH, N_RES),
        in_specs=[res_row(D_MODEL),
                  pl.BlockSpec((1, 8, 1, SLAB, LANES), lambda b, r: (b, 0, r, 0, 0)),
                  pl.BlockSpec((1, SLAB, D_MODEL), lambda b, r: (b * N_RES + r, 0, 0)),
                  res_row(PLE_DIM),
                  _const_spec(wo.shape), _const_spec(wg.shape), _const_spec(wp.shape)],
        out_specs=res_row(D_MODEL),
        out_shape=jax.ShapeDtypeStruct((BATCH, SLAB, N_RES * D_MODEL), f32),
        scratch_shapes=[pltpu.VMEM((SLAB, D_MODEL), bf16), pltpu.VMEM((SLAB, D_MODEL), f32)],
        compiler_params=pltpu.CompilerParams(dimension_semantics=("arbitrary", "arbitrary"),
                                             vmem_limit_bytes=VMEM_LIMIT),
        name="post_odd",
    )(h_res, o, z, p_res, wo, wg, wp)


def _tile4(g):
    return jnp.tile(g.astype(f32), CHUNK // g.shape[0])


def kernel(x, p, positions, norm_g, w_in_even, b_forget, qn_a, kn_a, qn_b, kn_b, lam_q1, lam_k1,
           lam_q2, lam_k2, subln_g, w_out_even, w_in_odd, qn_c, kn_c, w_out_odd, w_ple, w_ple_gate):
    tm = 512
    tq_diff, tq_fox = 512, 512
    n_tok = BATCH * SEQ
    x2d = x.reshape(n_tok, D_MODEL)
    pos_f = positions.astype(f32)

    w0 = w_in_even[0]
    f_lo, f_hi = 3072, 3076
    w_main0 = jnp.concatenate([w0[:, :f_lo], w0[:, f_hi:]], axis=1).astype(bf16)
    w_f = jnp.pad(w0[:, f_lo:f_hi], ((0, 0), (0, LANES - 4))).astype(bf16)
    b_f = jnp.pad(b_forget[0].astype(f32), (0, LANES - 4)).reshape(1, LANES)
    ones_row = jnp.ones((CHUNK,), f32)
    gains0 = jnp.stack([_tile4(qn_a[0]) * (64 ** -0.5 * LOG2E), _tile4(kn_a[0]), ones_row,
                        _tile4(qn_b[0]) * (HEAD_DIM ** -0.5 * LOG2E), _tile4(kn_b[0]), ones_row,
                        ones_row, ones_row])
    qa, ka, va, qb, kb, vb, z0, logf = _proj_even(
        x2d, pos_f.reshape(n_tok, 1), norm_g[0].reshape(1, D_MODEL).astype(f32), w_main0, gains0,
        w_f, b_f, tm)
    cum = _forget_cumsum(logf)

    lam_rows = jnp.pad(jnp.stack([lam_q1[0], lam_k1[0], lam_q2[0], lam_k2[0]]).astype(f32),
                       ((0, 4), (0, LANES - 64)))
    lam_init = 0.8 - 0.6 * math.exp(-0.3 * 0)
    oa = _layer0_attention(
        qa, ka, va, [lam_rows, subln_g[0].reshape(LANES, 1).astype(f32)],
        [_const_spec((8, LANES)), _const_spec((LANES, 1))],
        functools.partial(_diff_attn_kernel, tq=tq_diff, lam_init=lam_init),
        2, tq_diff, "diff_attn")
    ob = _layer0_attention(
        qb, kb, vb, [cum], [pl.BlockSpec((1, 1, SEQ, LANES), lambda b, h, i: (b, h, 0, 0))],
        functools.partial(_fox_attn_kernel, tq=tq_fox), 1, tq_fox, "fox_attn")
    h1 = _post_even(x2d, oa, ob, z0, p[0].reshape(n_tok, PLE_DIM), w_out_even[0].astype(bf16),
                    w_ple_gate[0].astype(bf16), w_ple[0].astype(bf16), tm)

    h_res = h1.reshape(BATCH, SLAB, N_RES * D_MODEL)
    pos_res = pos_f.reshape(BATCH, SLAB, N_RES).transpose(0, 2, 1).reshape(BATCH, N_RES, SLAB, 1)
    ones_row2 = jnp.ones((CHUNK,), f32)
    gq = _tile4(qn_c[0]) * (HEAD_DIM ** -0.5)
    gk = _tile4(kn_c[0])
    gains1 = jnp.stack([gq, gq, gk, gk, ones_row2, ones_row2, ones_row2, ones_row2])
    q1, k1, v1, z1 = _proj_odd(h_res, pos_res, norm_g[1].reshape(1, D_MODEL).astype(f32),
                               w_in_odd[0].astype(bf16), gains1)
    o1 = _dilated_attention(q1, k1, v1)
    out = _post_odd(h_res, o1, z1, p[1].reshape(BATCH, SLAB, N_RES * PLE_DIM),
                    w_out_odd[0].astype(bf16), w_ple_gate[1].astype(bf16), w_ple[1].astype(bf16))
    return out.reshape(BATCH, SEQ, D_MODEL)
```

```python
import functools
import math

import numpy as np
import jax
import jax.numpy as jnp
from jax import lax
from jax.experimental import pallas as pl
from jax.experimental.pallas import tpu as pltpu

D_MODEL = 1024
BATCH = 8
SEQ = 4096
PLE_DIM = 256
HEAD_DIM = 128
ROPE_THETA = 500000.0
RMS_EPS = 1e-6
LANES = 128
CHUNK = 512
N_RES = 16
SLAB = SEQ // N_RES
VMEM_LIMIT = 56 * 1024 * 1024
NEG = -1e30
LOG2E = math.log2(math.e)
UNROLL_16, UNROLL_4, UNROLL_1 = 4, 2, 4

bf16 = jnp.bfloat16
f32 = jnp.float32


def _nt_dot(a, b):
    return lax.dot_general(a, b, (((1,), (1,)), ((), ())), preferred_element_type=f32)


def _sigmoid(x):
    return 1.0 / (1.0 + jnp.exp(-x))


def _proj_kernel(x_ref, pos_ref, ng_ref, w_ref, gains_ref, ones_a_ref, ones_b_ref, lanec_ref,
                 *rest, chunks, rope_half, has_f):
    if has_f:
        wf_ref, bf_ref = rest[0], rest[1]
        out_refs = rest[2:]
    else:
        out_refs = rest
    x = x_ref[...].reshape(x_ref.shape[-2], x_ref.shape[-1])
    ms = jnp.mean(x * x, axis=-1, keepdims=True)
    hn = (x * lax.rsqrt(ms + RMS_EPS) * ng_ref[...]).astype(bf16)

    pos = pos_ref[...].reshape(x.shape[0], 1)
    ang = pos * lanec_ref[0:1, :]
    cosv = jnp.cos(ang)
    sinv = jnp.sin(ang)
    sin_hi = sinv * lanec_ref[1:2, :]
    sin_lo = sinv * lanec_ref[2:3, :]

    for c, (kind, group, oi, slot) in enumerate(chunks):
        acc = jnp.dot(hn, w_ref[:, c * CHUNK:(c + 1) * CHUNK], preferred_element_type=f32)
        if kind in ("rope", "norm"):
            ones_ref = ones_a_ref if group == 64 else ones_b_ref
            ss = jnp.dot((acc * acc).astype(bf16), ones_ref[...], preferred_element_type=f32)
            acc = acc * lax.rsqrt(ss * (1.0 / group) + RMS_EPS) * gains_ref[c:c + 1, :]
        o_ref = out_refs[oi]
        for hh in range(CHUNK // LANES):
            yb = acc[:, hh * LANES:(hh + 1) * LANES]
            if kind == "rope":
                yb = (yb * cosv + pltpu.roll(yb, rope_half, 1) * sin_hi
                      + pltpu.roll(yb, LANES - rope_half, 1) * sin_lo)
            if kind == "plain_t":
                o_ref[0, slot * (CHUNK // LANES) + hh] = yb.T.astype(bf16)
                continue
            yb = yb.astype(bf16)
            if kind == "z":
                col = slot * CHUNK + hh * LANES
                if len(o_ref.shape) == 3:
                    o_ref[0, :, col:col + LANES] = yb
                else:
                    o_ref[:, col:col + LANES] = yb
            else:
                head = slot * (CHUNK // LANES) + hh
                if len(o_ref.shape) == 5:
                    o_ref[0, head, 0] = yb
                else:
                    o_ref[0, head] = yb

    if has_f:
        f = jnp.dot(hn, wf_ref[...], preferred_element_type=f32) + bf_ref[...]
        out_refs[-1][...] = jnp.minimum(f, 0.0) - jnp.log1p(jnp.exp(-jnp.abs(f)))


def _lane_consts(group, half):
    lane = np.arange(LANES) % group
    inv_freq = jnp.exp(-math.log(ROPE_THETA) * jnp.arange(half, dtype=f32) / half)
    invf = jnp.where(lane < 2 * half, inv_freq[lane % half], 0.0)
    hi = ((lane >= half) & (lane < 2 * half)).astype(np.float32)
    lo = -(lane < half).astype(np.float32)
    rows = jnp.stack([invf, jnp.asarray(hi), jnp.asarray(lo)])
    return jnp.concatenate([rows, jnp.zeros((5, LANES), f32)], axis=0)


def _block_ones(group):
    i = np.arange(CHUNK)
    return jnp.asarray((i[:, None] // group == i[None, :] // group).astype(np.float32), dtype=bf16)


def _const_spec(shape):
    nd = len(shape)
    return pl.BlockSpec(shape, lambda *_: (0,) * nd)


def _proj_even(x2d, pos_col, ng, w_main, gains, w_f, b_f, tm):
    n_t = (BATCH * SEQ) // tm
    per_b = SEQ // tm
    chunks = (("rope", 64, 0, 0), ("rope", 64, 1, 0), ("plain_t", 0, 2, 0),
              ("norm", 128, 3, 0), ("norm", 128, 4, 0), ("plain_t", 0, 5, 0),
              ("z", 0, 6, 0), ("z", 0, 6, 1))
    head_shape = jax.ShapeDtypeStruct((BATCH, 4, SEQ, LANES), bf16)
    head_spec = pl.BlockSpec((1, 4, tm, LANES), lambda i: (i // per_b, 0, i % per_b, 0))
    head_t_shape = jax.ShapeDtypeStruct((BATCH, 4, LANES, SEQ), bf16)
    head_t_spec = pl.BlockSpec((1, 4, LANES, tm), lambda i: (i // per_b, 0, 0, i % per_b))
    row_spec = lambda w: pl.BlockSpec((tm, w), lambda i: (i, 0))
    kern = functools.partial(_proj_kernel, chunks=chunks, rope_half=8, has_f=True)
    return pl.pallas_call(
        kern,
        grid=(n_t,),
        in_specs=[row_spec(D_MODEL), row_spec(1), _const_spec((1, D_MODEL)),
                  _const_spec(w_main.shape), _const_spec((8, CHUNK)),
                  _const_spec((CHUNK, CHUNK)), _const_spec((CHUNK, CHUNK)),
                  _const_spec((8, LANES)), _const_spec((D_MODEL, LANES)), _const_spec((1, LANES))],
        out_specs=[head_spec, head_spec, head_t_spec] * 2 + [row_spec(D_MODEL), row_spec(LANES)],
        out_shape=[head_shape, head_shape, head_t_shape] * 2 + [
            jax.ShapeDtypeStruct((BATCH * SEQ, D_MODEL), bf16),
            jax.ShapeDtypeStruct((BATCH * SEQ, LANES), f32)],
        compiler_params=pltpu.CompilerParams(dimension_semantics=("arbitrary",),
                                             vmem_limit_bytes=VMEM_LIMIT),
        name="proj_even",
    )(x2d, pos_col, ng, w_main, gains, _block_ones(64), _block_ones(128), _lane_consts(64, 8),
      w_f, b_f)


def _proj_odd(h_res, pos_res, ng, w_main, gains):
    chunks = (("rope", 128, 0, 0), ("rope", 128, 0, 1), ("rope", 128, 1, 0), ("rope", 128, 1, 1),
              ("plain", 0, 2, 0), ("plain", 0, 2, 1), ("z", 0, 3, 0), ("z", 0, 3, 1))
    head_shape = jax.ShapeDtypeStruct((BATCH, 8, N_RES, SLAB, LANES), bf16)
    head_spec = pl.BlockSpec((1, 8, 1, SLAB, LANES), lambda b, r: (b, 0, r, 0, 0))
    kern = functools.partial(_proj_kernel, chunks=chunks, rope_half=16, has_f=False)
    return pl.pallas_call(
        kern,
        grid=(BATCH, N_RES),
        in_specs=[pl.BlockSpec((1, SLAB, D_MODEL), lambda b, r: (b, 0, r)),
                  pl.BlockSpec((1, 1, SLAB, 1), lambda b, r: (b, r, 0, 0)),
                  _const_spec((1, D_MODEL)), _const_spec(w_main.shape), _const_spec((8, CHUNK)),
                  _const_spec((CHUNK, CHUNK)), _const_spec((CHUNK, CHUNK)),
                  _const_spec((8, LANES))],
        out_specs=[head_spec] * 3 + [pl.BlockSpec((1, SLAB, D_MODEL), lambda b, r: (b * N_RES + r, 0, 0))],
        out_shape=[head_shape] * 3 + [jax.ShapeDtypeStruct((BATCH * N_RES, SLAB, D_MODEL), bf16)],
        compiler_params=pltpu.CompilerParams(dimension_semantics=("arbitrary", "arbitrary"),
                                             vmem_limit_bytes=VMEM_LIMIT),
        name="proj_odd",
    )(h_res, pos_res, ng, w_main, gains, _block_ones(64), _block_ones(128), _lane_consts(128, 16))


def _split3(x):
    hi = x.astype(bf16).astype(f32)
    r1 = x - hi
    mid = r1.astype(bf16).astype(f32)
    lo = (r1 - mid).astype(bf16).astype(f32)
    return hi, mid, lo


def _lane_place(lane, base, terms, init):
    out = init
    for n, t in enumerate(terms):
        out = jnp.where(lane == base + n, t, out)
    return out


def _cumsum_kernel(logf_ref, aug_ref, *, blk):
    row = lax.broadcasted_iota(jnp.int32, (blk, blk), 0)
    col = lax.broadcasted_iota(jnp.int32, (blk, blk), 1)
    lower = (col <= row).astype(bf16)
    lane = lax.broadcasted_iota(jnp.int32, (blk, LANES), 1)
    carry = jnp.zeros((1, LANES), f32)
    for i in range(SEQ // blk):
        x = logf_ref[i * blk:(i + 1) * blk, :]
        packed = jnp.zeros((blk, LANES), f32)
        for hh in range(4):
            xh = jnp.broadcast_to(x[:, hh:hh + 1], (blk, LANES))
            packed = _lane_place(lane, 3 * hh, _split3(xh), packed)
        cs = jnp.dot(lower, packed.astype(bf16), preferred_element_type=f32) + carry
        carry = cs[blk - 1:blk, :]
        for hh in range(4):
            c = (cs[:, 3 * hh:3 * hh + 1] + cs[:, 3 * hh + 1:3 * hh + 2]) + cs[:, 3 * hh + 2:3 * hh + 3]
            cb = jnp.broadcast_to(c * (-LOG2E), (blk, LANES))
            aug_ref[0, hh, i * blk:(i + 1) * blk, :] = _lane_place(
                lane, 0, _split3(cb), jnp.zeros((blk, LANES), f32)).astype(bf16)


def _forget_cumsum(logf):
    return pl.pallas_call(
        functools.partial(_cumsum_kernel, blk=512),
        grid=(BATCH,),
        in_specs=[pl.BlockSpec((SEQ, LANES), lambda b: (b, 0))],
        out_specs=pl.BlockSpec((1, 4, SEQ, LANES), lambda b: (b, 0, 0, 0)),
        out_shape=jax.ShapeDtypeStruct((BATCH, 4, SEQ, LANES), bf16),
        compiler_params=pltpu.CompilerParams(dimension_semantics=("arbitrary",),
                                             vmem_limit_bytes=VMEM_LIMIT),
        name="forget_cumsum",
    )(logf)


def _flash_sweep(qq, k_ref, vt_ref, aug_ref, acc_sc, m_sc, l_sc, i, t):
    n = qq.shape[0]
    m_sc[...] = jnp.full(m_sc.shape, -jnp.inf, f32)
    l_sc[...] = jnp.zeros(l_sc.shape, f32)
    acc_sc[...] = jnp.zeros(acc_sc.shape, f32)

    def scores(j):
        k = k_ref[0, 0, pl.ds(j * t, t), :]
        if aug_ref is not None:
            k = jnp.concatenate([k, aug_ref[0, 0, pl.ds(j * t, t), :]], axis=1)
        return _nt_dot(k, qq)

    def update(s, j, diag):
        if diag:
            key = lax.broadcasted_iota(jnp.int32, s.shape, 0)
            qry = lax.broadcasted_iota(jnp.int32, s.shape, 1)
            if n > t:
                qry = jnp.where(qry >= t, qry - t, qry)
            s = jnp.where(key <= qry, s, -jnp.inf)
        vt = vt_ref[0, 0, :, pl.ds(j * t, t)]
        m_prev = m_sc[...]
        m_new = jnp.maximum(m_prev, jnp.max(s, axis=0, keepdims=True))
        alpha = jnp.exp2(m_prev - m_new)
        p = jnp.exp2(s - m_new)
        l_sc[...] = alpha * l_sc[...] + jnp.sum(p, axis=0, keepdims=True)
        acc_sc[...] = alpha * acc_sc[...] + jnp.dot(vt, p.astype(bf16), preferred_element_type=f32)
        m_sc[...] = m_new

    def pair(j0, diag1):
        s0 = scores(j0)
        s1 = scores(j0 + 1)
        update(s0, j0, False)
        update(s1, j0 + 1, diag1)

    def body(pidx, carry):
        pair(2 * pidx, False)
        return carry

    lax.fori_loop(0, i // 2, body, 0)

    @pl.when(i % 2 == 1)
    def _():
        pair(i - 1, True)

    @pl.when(i % 2 == 0)
    def _():
        update(scores(i), i, True)


def _diff_attn_kernel(q_ref, k_ref, vt_ref, lam_ref, g_ref, o_ref, acc_sc, m_sc, l_sc, *,
                      tq, lam_init):
    i = pl.program_id(2)
    q = q_ref[0, 0]
    lane = lax.broadcasted_iota(jnp.int32, q.shape, 1)
    zero = jnp.zeros_like(q)
    qq = jnp.concatenate([jnp.where(lane < 64, q, zero), jnp.where(lane >= 64, q, zero)], axis=0)
    _flash_sweep(qq, k_ref, vt_ref, None, acc_sc, m_sc, l_sc, i, tq)
    lam = (jnp.exp(jnp.sum(lam_ref[0:1, :] * lam_ref[1:2, :], axis=1, keepdims=True))
           - jnp.exp(jnp.sum(lam_ref[2:3, :] * lam_ref[3:4, :], axis=1, keepdims=True)) + lam_init)
    ot = acc_sc[...] / l_sc[...]
    ot = ot[:, 0:tq] - lam * ot[:, tq:2 * tq]
    ms = jnp.mean(ot * ot, axis=0, keepdims=True)
    ot = ot * lax.rsqrt(ms + RMS_EPS) * g_ref[...] * (1.0 - lam_init)
    o_ref[0, 0] = ot.T.astype(bf16)


def _fox_attn_kernel(q_ref, k_ref, vt_ref, aug_ref, o_ref, acc_sc, m_sc, l_sc, *, tq):
    i = pl.program_id(2)
    q = q_ref[0, 0]
    lane = lax.broadcasted_iota(jnp.int32, q.shape, 1)
    ones3 = jnp.where(lane < 3, 1.0, 0.0).astype(bf16)
    qq = jnp.concatenate([q, ones3], axis=1)
    _flash_sweep(qq, k_ref, vt_ref, aug_ref, acc_sc, m_sc, l_sc, i, tq)
    o_ref[0, 0] = (acc_sc[...] / l_sc[...]).T.astype(bf16)


def _layer0_attention(q, k, vt, extra, extra_specs, kern, n_stack, tq, name):
    nq = SEQ // tq
    q_spec = pl.BlockSpec((1, 1, tq, LANES), lambda b, h, i: (b, h, i, 0))
    k_spec = pl.BlockSpec((1, 1, SEQ, LANES), lambda b, h, i: (b, h, 0, 0))
    vt_spec = pl.BlockSpec((1, 1, LANES, SEQ), lambda b, h, i: (b, h, 0, 0))
    n = n_stack * tq
    return pl.pallas_call(
        kern,
        grid=(BATCH, 4, nq),
        in_specs=[q_spec, k_spec, vt_spec] + extra_specs,
        out_specs=q_spec,
        out_shape=jax.ShapeDtypeStruct((BATCH, 4, SEQ, LANES), bf16),
        scratch_shapes=[pltpu.VMEM((LANES, n), f32), pltpu.VMEM((1, n), f32), pltpu.VMEM((1, n), f32)],
        compiler_params=pltpu.CompilerParams(
            dimension_semantics=("arbitrary", "arbitrary", "arbitrary"), vmem_limit_bytes=VMEM_LIMIT),
        name=name,
    )(q, k, vt, *extra)


def _pattern_bias(dil):
    n_c = N_RES // dil
    bq = max(128 // n_c, 16)
    bk = 2 * bq
    cq, iq = np.divmod(np.arange(n_c * bq), bq)
    ck, jk = np.divmod(np.arange(n_c * bk), bk)
    out = []
    for shift in (0, bq):
        dist = n_c * (shift + iq[:, None] - jk[None, :]) + (cq[:, None] - ck[None, :])
        out.append(np.where((dist >= 0) & (dist <= 128), 0.0, NEG))
    return jnp.asarray(np.stack(out), dtype=f32), n_c, bq


def _dilated_kernel(q_ref, k_ref, v_ref, b16_ref, b4_ref, b1_ref, o_ref, acc_sc, m_sc, l_sc, *, geo):
    def pieces(ref, lead, slabs, start, n):
        parts = [ref[lead + (s, pl.ds(start, n), slice(None))] for s in slabs]
        return parts[0] if len(parts) == 1 else jnp.concatenate(parts, axis=0)

    def load(slabs, ub, bq, bias_ref, first):
        u0 = ub * bq
        ks = jnp.maximum(ub - 1, 0) * bq
        t = dict(slabs=slabs, u0=u0, bq=bq, first=first,
                 q=pieces(q_ref, (0, 0), slabs, u0, bq),
                 k=pieces(k_ref, (0, 0), slabs, ks, 2 * bq),
                 v=pieces(v_ref, (0, 0), slabs, ks, 2 * bq),
                 bias=bias_ref[jnp.minimum(ub, 1)])
        if not first:
            t["m"] = pieces(m_sc, (), slabs, u0, bq)
            t["l"] = pieces(l_sc, (), slabs, u0, bq)
            t["a"] = pieces(acc_sc, (), slabs, u0, bq)
        return t

    def compute(t):
        s = _nt_dot(t["q"], t["k"]) + t["bias"]
        m_cur = jnp.max(s, axis=1, keepdims=True)
        v_ext = jnp.concatenate([t["v"], jnp.ones(t["v"].shape, bf16)], axis=1)
        if t["first"]:
            m_new = jnp.broadcast_to(m_cur, (s.shape[0], LANES))
        else:
            m_new = jnp.maximum(t["m"], m_cur)
            alpha = jnp.exp2(t["m"] - m_new)
        p = jnp.exp2(s - jnp.concatenate([m_new] * (s.shape[1] // LANES), axis=1))
        ext = jnp.dot(p.astype(bf16), v_ext, preferred_element_type=f32)
        a_new, l_new = ext[:, :LANES], ext[:, LANES:]
        if not t["first"]:
            l_new = l_new + alpha * t["l"]
            a_new = a_new + alpha * t["a"]
        return m_new, l_new, a_new

    def store(t, res):
        bq, u0 = t["bq"], t["u0"]
        for n, s_id in enumerate(t["slabs"]):
            rows = slice(n * bq, (n + 1) * bq)
            for ref, val in zip((m_sc, l_sc, acc_sc), res):
                ref[s_id, pl.ds(u0, bq), :] = val[rows]

    def run(tiles):
        results = [compute(t) for t in tiles]
        for t, res in zip(tiles, results):
            store(t, res)

    first = True
    for (dil, n_c, bq), bias_ref in zip(geo, (b16_ref, b4_ref, b1_ref)):
        n_tiles = SLAB // bq
        if n_c == 1:
            def body(rg, carry, bq=bq, bias_ref=bias_ref, first=first, n_tiles=n_tiles):
                run([load([UNROLL_16 * rg + dr], ub, bq, bias_ref, first)
                     for dr in range(UNROLL_16) for ub in range(n_tiles)])
                return carry
            lax.fori_loop(0, N_RES // UNROLL_16, body, 0)
        elif n_c == N_RES:
            def body(pp, carry, bq=bq, bias_ref=bias_ref, first=first):
                run([load(list(range(N_RES)), UNROLL_1 * pp + du, bq, bias_ref, first)
                     for du in range(UNROLL_1)])
                return carry
            lax.fori_loop(0, n_tiles // UNROLL_1, body, 0)
        else:
            def body(up, carry, dil=dil, n_c=n_c, bq=bq, bias_ref=bias_ref, first=first):
                run([load([c * dil + res for c in range(n_c)], UNROLL_4 * up + du, bq, bias_ref, first)
                     for du in range(UNROLL_4) for res in range(dil)])
                return carry
            lax.fori_loop(0, n_tiles // UNROLL_4, body, 0)
        first = False

    def fin(r, carry):
        o_ref[0, 0, r] = (acc_sc[r] / l_sc[r]).astype(bf16)
        return carry
    lax.fori_loop(0, N_RES, fin, 0)


def _dilated_attention(q, k, v):
    biases, geo = [], []
    for dil in (16, 4, 1):
        b, n_c, bq = _pattern_bias(dil)
        biases.append(b)
        geo.append((dil, n_c, bq))
    spec = pl.BlockSpec((1, 1, N_RES, SLAB, LANES), lambda b, h: (b, h, 0, 0, 0))
    return pl.pallas_call(
        functools.partial(_dilated_kernel, geo=tuple(geo)),
        grid=(BATCH, 8),
        in_specs=[spec, spec, spec] + [_const_spec(b.shape) for b in biases],
        out_specs=spec,
        out_shape=jax.ShapeDtypeStruct((BATCH, 8, N_RES, SLAB, LANES), bf16),
        scratch_shapes=[pltpu.VMEM((N_RES, SLAB, LANES), f32)] * 3,
        compiler_params=pltpu.CompilerParams(dimension_semantics=("arbitrary", "arbitrary"),
                                             vmem_limit_bytes=VMEM_LIMIT),
        name="dilated_attn",
    )(q, k, v, *biases)


def _post_kernel(h_ref, *rest, n_o):
    o_refs = rest[:n_o]
    z_ref, p_ref, wo_ref, wg_ref, wp_ref, out_ref, mixed_sc, h1_sc = rest[n_o:]
    tm = mixed_sc.shape[0]
    heads_per = D_MODEL // LANES // n_o
    for n, o_ref in enumerate(o_refs):
        for hh in range(heads_per):
            col = (n * heads_per + hh) * LANES
            o = (o_ref[0, hh] if len(o_ref.shape) == 4 else o_ref[0, hh, 0]).astype(f32)
            z = (z_ref[0, :, col:col + LANES] if len(z_ref.shape) == 3
                 else z_ref[:, col:col + LANES]).astype(f32)
            mixed_sc[:, col:col + LANES] = (o * (z * _sigmoid(z))).astype(bf16)
    h = h_ref[...].reshape(tm, D_MODEL)
    h1 = h + jnp.dot(mixed_sc[...], wo_ref[...], preferred_element_type=f32)
    h1_sc[...] = h1
    h1b = h1.astype(bf16)
    pb = p_ref[...].reshape(tm, PLE_DIM).astype(bf16)
    for c in range(D_MODEL // CHUNK):
        cols = slice(c * CHUNK, (c + 1) * CHUNK)
        gate = _sigmoid(jnp.dot(h1b, wg_ref[:, cols], preferred_element_type=f32))
        emb = jnp.dot(pb, wp_ref[:, cols], preferred_element_type=f32)
        res = h1_sc[:, cols] + emb * gate
        if len(out_ref.shape) == 3:
            out_ref[0, :, cols] = res
        else:
            out_ref[:, cols] = res


def _post_even(x2d, oa, ob, z, p0, wo, wg, wp, tm):
    n_t = (BATCH * SEQ) // tm
    per_b = SEQ // tm
    row_spec = lambda w: pl.BlockSpec((tm, w), lambda i: (i, 0))
    head_spec = pl.BlockSpec((1, 4, tm, LANES), lambda i: (i // per_b, 0, i % per_b, 0))
    return pl.pallas_call(
        functools.partial(_post_kernel, n_o=2),
        grid=(n_t,),
        in_specs=[row_spec(D_MODEL), head_spec, head_spec, row_spec(D_MODEL), row_spec(PLE_DIM),
                  _const_spec(wo.shape), _const_spec(wg.shape), _const_spec(wp.shape)],
        out_specs=row_spec(D_MODEL),
        out_shape=jax.ShapeDtypeStruct((BATCH * SEQ, D_MODEL), f32),
        scratch_shapes=[pltpu.VMEM((tm, D_MODEL), bf16), pltpu.VMEM((tm, D_MODEL), f32)],
        compiler_params=pltpu.CompilerParams(dimension_semantics=("arbitrary",),
                                             vmem_limit_bytes=VMEM_LIMIT),
        name="post_even",
    )(x2d, oa, ob, z, p0, wo, wg, wp)


def _post_odd(h_res, o, z, p_res, wo, wg, wp):
    res_row = lambda w: pl.BlockSpec((1, SLAB, w), lambda b, r: (b, 0, r))
    return pl.pallas_call(
        functools.partial(_post_kernel, n_o=1),
        grid=(BATCH, N_RES),
        in_specs=[res_row(D_MODEL),
                  pl.BlockSpec((1, 8, 1, SLAB, LANES), lambda b, r: (b, 0, r, 0, 0)),
                  pl.BlockSpec((1, SLAB, D_MODEL), lambda b, r: (b * N_RES + r, 0, 0)),
                  res_row(PLE_DIM),
                  _const_spec(wo.shape), _const_spec(wg.shape), _const_spec(wp.shape)],
        out_specs=res_row(D_MODEL),
        out_shape=jax.ShapeDtypeStruct((BATCH, SLAB, N_RES * D_MODEL), f32),
        scratch_shapes=[pltpu.VMEM((SLAB, D_MODEL), bf16), pltpu.VMEM((SLAB, D_MODEL), f32)],
        compiler_params=pltpu.CompilerParams(dimension_semantics=("arbitrary", "arbitrary"),
                                             vmem_limit_bytes=VMEM_LIMIT),
        name="post_odd",
    )(h_res, o, z, p_res, wo, wg, wp)


def _tile4(g):
    return jnp.tile(g.astype(f32), CHUNK // g.shape[0])


def kernel(x, p, positions, norm_g, w_in_even, b_forget, qn_a, kn_a, qn_b, kn_b, lam_q1, lam_k1,
           lam_q2, lam_k2, subln_g, w_out_even, w_in_odd, qn_c, kn_c, w_out_odd, w_ple, w_ple_gate):
    tm = 512
    tq_diff, tq_fox = 512, 512
    n_tok = BATCH * SEQ
    x2d = x.reshape(n_tok, D_MODEL)
    pos_f = positions.astype(f32)

    w0 = w_in_even[0]
    f_lo, f_hi = 3072, 3076
    w_main0 = jnp.concatenate([w0[:, :f_lo], w0[:, f_hi:]], axis=1).astype(bf16)
    w_f = jnp.pad(w0[:, f_lo:f_hi], ((0, 0), (0, LANES - 4))).astype(bf16)
    b_f = jnp.pad(b_forget[0].astype(f32), (0, LANES - 4)).reshape(1, LANES)
    ones_row = jnp.ones((CHUNK,), f32)
    gains0 = jnp.stack([_tile4(qn_a[0]) * (64 ** -0.5 * LOG2E), _tile4(kn_a[0]), ones_row,
                        _tile4(qn_b[0]) * (HEAD_DIM ** -0.5 * LOG2E), _tile4(kn_b[0]), ones_row,
                        ones_row, ones_row])
    qa, ka, va, qb, kb, vb, z0, logf = _proj_even(
        x2d, pos_f.reshape(n_tok, 1), norm_g[0].reshape(1, D_MODEL).astype(f32), w_main0, gains0,
        w_f, b_f, tm)
    cum = _forget_cumsum(logf)

    lam_rows = jnp.pad(jnp.stack([lam_q1[0], lam_k1[0], lam_q2[0], lam_k2[0]]).astype(f32),
                       ((0, 4), (0, LANES - 64)))
    lam_init = 0.8 - 0.6 * math.exp(-0.3 * 0)
    oa = _layer0_attention(
        qa, ka, va, [lam_rows, subln_g[0].reshape(LANES, 1).astype(f32)],
        [_const_spec((8, LANES)), _const_spec((LANES, 1))],
        functools.partial(_diff_attn_kernel, tq=tq_diff, lam_init=lam_init),
        2, tq_diff, "diff_attn")
    ob = _layer0_attention(
        qb, kb, vb, [cum], [pl.BlockSpec((1, 1, SEQ, LANES), lambda b, h, i: (b, h, 0, 0))],
        functools.partial(_fox_attn_kernel, tq=tq_fox), 1, tq_fox, "fox_attn")
    h1 = _post_even(x2d, oa, ob, z0, p[0].reshape(n_tok, PLE_DIM), w_out_even[0].astype(bf16),
                    w_ple_gate[0].astype(bf16), w_ple[0].astype(bf16), tm)

    h_res = h1.reshape(BATCH, SLAB, N_RES * D_MODEL)
    pos_res = pos_f.reshape(BATCH, SLAB, N_RES).transpose(0, 2, 1).reshape(BATCH, N_RES, SLAB, 1)
    ones_row2 = jnp.ones((CHUNK,), f32)
    gq = _tile4(qn_c[0]) * (HEAD_DIM ** -0.5 * LOG2E)
    gk = _tile4(kn_c[0])
    gains1 = jnp.stack([gq, gq, gk, gk, ones_row2, ones_row2, ones_row2, ones_row2])
    q1, k1, v1, z1 = _proj_odd(h_res, pos_res, norm_g[1].reshape(1, D_MODEL).astype(f32),
                               w_in_odd[0].astype(bf16), gains1)
    o1 = _dilated_attention(q1, k1, v1)
    out = _post_odd(h_res, o1, z1, p[1].reshape(BATCH, SLAB, N_RES * PLE_DIM),
                    w_out_odd[0].astype(bf16), w_ple_gate[1].astype(bf16), w_ple[1].astype(bf16))
    return out.reshape(BATCH, SEQ, D_MODEL)
```

```python
import functools
import math

import numpy as np
import jax
import jax.numpy as jnp
from jax import lax
from jax.experimental import pallas as pl
from jax.experimental.pallas import tpu as pltpu

D_MODEL = 1024
BATCH = 8
SEQ = 4096
PLE_DIM = 256
HEAD_DIM = 128
ROPE_THETA = 500000.0
RMS_EPS = 1e-6
LANES = 128
CHUNK = 512
N_RES = 16
SLAB = SEQ // N_RES
RES_ROWS = 64
VMEM_LIMIT = 56 * 1024 * 1024
NEG = -1e30
LOG2E = math.log2(math.e)
UNROLL_16, UNROLL_4, UNROLL_1 = 4, 2, 4

bf16 = jnp.bfloat16
f32 = jnp.float32


def _nt_dot(a, b):
    return lax.dot_general(a, b, (((1,), (1,)), ((), ())), preferred_element_type=f32)


def _sigmoid(x):
    return 1.0 / (1.0 + jnp.exp(-x))


def _proj_kernel(x_ref, pos_ref, ng_ref, w_ref, gains_ref, ones_a_ref, ones_b_ref, invf_ref,
                 *rest, chunks, rope_half, rope_group, has_f, res_rows):
    if has_f:
        wf_ref, bf_ref = rest[0], rest[1]
        out_refs = rest[2:]
    else:
        out_refs = rest
    if res_rows is None:
        x = x_ref[...]
    else:
        x = jnp.concatenate([x_ref[0, :, rr, :] for rr in range(8)], axis=0)
    ms = jnp.mean(x * x, axis=-1, keepdims=True)
    hn = (x * lax.rsqrt(ms + RMS_EPS) * ng_ref[...]).astype(bf16)

    tm = x.shape[0]
    ang = invf_ref[...] * pos_ref[...].reshape(1, tm)
    cos_c, sin_c = jnp.cos(ang), jnp.sin(ang)
    zeros_c = jnp.zeros_like(ang)
    rest_rows = rope_group - 2 * rope_half
    reps = LANES // rope_group

    def lane_table(first, second, fill):
        blk = [first, second, jnp.full((rest_rows, tm), fill, f32)]
        return jnp.concatenate(blk * reps, axis=0).T

    cosv = lane_table(cos_c, cos_c, 1.0)
    sin_hi = lane_table(zeros_c, sin_c, 0.0)
    sin_lo = lane_table(-sin_c, zeros_c, 0.0)

    def put_head(o_ref, head, yb):
        if res_rows is None:
            o_ref[0, head] = yb
        else:
            for rr in range(8):
                o_ref[0, head, rr] = yb[rr * res_rows:(rr + 1) * res_rows]

    def put_cols(o_ref, col, yb):
        if res_rows is None:
            o_ref[:, col:col + LANES] = yb
        else:
            for rr in range(8):
                o_ref[0, rr, :, col:col + LANES] = yb[rr * res_rows:(rr + 1) * res_rows]

    for c, (kind, group, oi, slot) in enumerate(chunks):
        acc = jnp.dot(hn, w_ref[:, c * CHUNK:(c + 1) * CHUNK], preferred_element_type=f32)
        if kind in ("rope", "norm"):
            ones_ref = ones_a_ref if group == 64 else ones_b_ref
            ss = jnp.dot((acc * acc).astype(bf16), ones_ref[...], preferred_element_type=f32)
            acc = acc * lax.rsqrt(ss * (1.0 / group) + RMS_EPS) * gains_ref[c:c + 1, :]
        o_ref = out_refs[oi]
        for hh in range(CHUNK // LANES):
            yb = acc[:, hh * LANES:(hh + 1) * LANES]
            if kind == "rope":
                yb = (yb * cosv + pltpu.roll(yb, rope_half, 1) * sin_hi
                      + pltpu.roll(yb, LANES - rope_half, 1) * sin_lo)
            if kind == "plain_t":
                o_ref[0, slot * (CHUNK // LANES) + hh] = yb.T.astype(bf16)
            elif kind == "z":
                put_cols(o_ref, slot * CHUNK + hh * LANES, yb.astype(bf16))
            else:
                put_head(o_ref, slot * (CHUNK // LANES) + hh, yb.astype(bf16))

    if has_f:
        f = jnp.dot(hn, wf_ref[...], preferred_element_type=f32) + bf_ref[...]
        out_refs[-1][...] = jnp.minimum(f, 0.0) - jnp.log1p(jnp.exp(-jnp.abs(f)))


def _inv_freq_col(half):
    return jnp.exp(-math.log(ROPE_THETA) * jnp.arange(half, dtype=f32) / half).reshape(half, 1)


def _block_ones(group):
    i = np.arange(CHUNK)
    return jnp.asarray((i[:, None] // group == i[None, :] // group).astype(np.float32), dtype=bf16)


def _const_spec(shape):
    nd = len(shape)
    return pl.BlockSpec(shape, lambda *_: (0,) * nd)


def _proj_even(x2d, pos_col, ng, w_main, gains, w_f, b_f, tm):
    n_t = (BATCH * SEQ) // tm
    per_b = SEQ // tm
    chunks = (("rope", 64, 0, 0), ("rope", 64, 1, 0), ("plain_t", 0, 2, 0),
              ("norm", 128, 3, 0), ("norm", 128, 4, 0), ("plain_t", 0, 5, 0),
              ("z", 0, 6, 0), ("z", 0, 6, 1))
    head_shape = jax.ShapeDtypeStruct((BATCH, 4, SEQ, LANES), bf16)
    head_spec = pl.BlockSpec((1, 4, tm, LANES), lambda i: (i // per_b, 0, i % per_b, 0))
    head_t_shape = jax.ShapeDtypeStruct((BATCH, 4, LANES, SEQ), bf16)
    head_t_spec = pl.BlockSpec((1, 4, LANES, tm), lambda i: (i // per_b, 0, 0, i % per_b))
    row_spec = lambda w: pl.BlockSpec((tm, w), lambda i: (i, 0))
    kern = functools.partial(_proj_kernel, chunks=chunks, rope_half=8, rope_group=64, has_f=True,
                             res_rows=None)
    return pl.pallas_call(
        kern,
        grid=(n_t,),
        in_specs=[row_spec(D_MODEL), pl.BlockSpec((1, 1, tm), lambda i: (i, 0, 0)),
                  _const_spec((1, D_MODEL)),
                  _const_spec(w_main.shape), _const_spec((8, CHUNK)),
                  _const_spec((CHUNK, CHUNK)), _const_spec((CHUNK, CHUNK)),
                  _const_spec((8, 1)), _const_spec((D_MODEL, LANES)), _const_spec((1, LANES))],
        out_specs=[head_spec, head_spec, head_t_spec] * 2 + [row_spec(D_MODEL), row_spec(LANES)],
        out_shape=[head_shape, head_shape, head_t_shape] * 2 + [
            jax.ShapeDtypeStruct((BATCH * SEQ, D_MODEL), bf16),
            jax.ShapeDtypeStruct((BATCH * SEQ, LANES), f32)],
        compiler_params=pltpu.CompilerParams(dimension_semantics=("arbitrary",),
                                             vmem_limit_bytes=VMEM_LIMIT),
        name="proj_even",
    )(x2d, pos_col, ng, w_main, gains, _block_ones(64), _block_ones(128), _inv_freq_col(8),
      w_f, b_f)


def _proj_odd(h_res, pos_res, ng, w_main, gains):
    chunks = (("rope", 128, 0, 0), ("rope", 128, 0, 1), ("rope", 128, 1, 0), ("rope", 128, 1, 1),
              ("plain", 0, 2, 0), ("plain", 0, 2, 1), ("z", 0, 3, 0), ("z", 0, 3, 1))
    head_shape = jax.ShapeDtypeStruct((BATCH, 8, N_RES, SLAB, LANES), bf16)
    head_spec = pl.BlockSpec((1, 8, 8, RES_ROWS, LANES), lambda b, g, c: (b, 0, g, c, 0))
    kern = functools.partial(_proj_kernel, chunks=chunks, rope_half=16, rope_group=128, has_f=False,
                             res_rows=RES_ROWS)
    n_g, n_c = N_RES // 8, SLAB // RES_ROWS
    return pl.pallas_call(
        kern,
        grid=(BATCH, n_g, n_c),
        in_specs=[pl.BlockSpec((1, RES_ROWS, 8, D_MODEL), lambda b, g, c: (b, c, g, 0)),
                  pl.BlockSpec((1, 1, 8 * RES_ROWS), lambda b, g, c: ((b * n_g + g) * n_c + c, 0, 0)),
                  _const_spec((1, D_MODEL)), _const_spec(w_main.shape), _const_spec((8, CHUNK)),
                  _const_spec((CHUNK, CHUNK)), _const_spec((CHUNK, CHUNK)),
                  _const_spec((16, 1))],
        out_specs=[head_spec] * 3 + [pl.BlockSpec((1, 8, RES_ROWS, D_MODEL), lambda b, g, c: (b, g, c, 0))],
        out_shape=[head_shape] * 3 + [jax.ShapeDtypeStruct((BATCH, N_RES, SLAB, D_MODEL), bf16)],
        compiler_params=pltpu.CompilerParams(dimension_semantics=("arbitrary",) * 3,
                                             vmem_limit_bytes=VMEM_LIMIT),
        name="proj_odd",
    )(h_res, pos_res, ng, w_main, gains, _block_ones(64), _block_ones(128), _inv_freq_col(16))


def _split3(x):
    hi = x.astype(bf16).astype(f32)
    r1 = x - hi
    mid = r1.astype(bf16).astype(f32)
    lo = (r1 - mid).astype(bf16).astype(f32)
    return hi, mid, lo


def _lane_place(lane, base, terms, init):
    out = init
    for n, t in enumerate(terms):
        out = jnp.where(lane == base + n, t, out)
    return out


def _cumsum_kernel(logf_ref, aug_ref, *, blk):
    row = lax.broadcasted_iota(jnp.int32, (blk, blk), 0)
    col = lax.broadcasted_iota(jnp.int32, (blk, blk), 1)
    lower = (col <= row).astype(bf16)
    lane = lax.broadcasted_iota(jnp.int32, (blk, LANES), 1)
    carry = jnp.zeros((1, LANES), f32)
    for i in range(SEQ // blk):
        x = logf_ref[i * blk:(i + 1) * blk, :]
        packed = jnp.zeros((blk, LANES), f32)
        for hh in range(4):
            xh = jnp.broadcast_to(x[:, hh:hh + 1], (blk, LANES))
            packed = _lane_place(lane, 3 * hh, _split3(xh), packed)
        cs = jnp.dot(lower, packed.astype(bf16), preferred_element_type=f32) + carry
        carry = cs[blk - 1:blk, :]
        for hh in range(4):
            c = (cs[:, 3 * hh:3 * hh + 1] + cs[:, 3 * hh + 1:3 * hh + 2]) + cs[:, 3 * hh + 2:3 * hh + 3]
            cb = jnp.broadcast_to(c * (-LOG2E), (blk, LANES))
            aug_ref[0, hh, i * blk:(i + 1) * blk, :] = _lane_place(
                lane, 0, _split3(cb), jnp.zeros((blk, LANES), f32)).astype(bf16)


def _forget_cumsum(logf):
    return pl.pallas_call(
        functools.partial(_cumsum_kernel, blk=512),
        grid=(BATCH,),
        in_specs=[pl.BlockSpec((SEQ, LANES), lambda b: (b, 0))],
        out_specs=pl.BlockSpec((1, 4, SEQ, LANES), lambda b: (b, 0, 0, 0)),
        out_shape=jax.ShapeDtypeStruct((BATCH, 4, SEQ, LANES), bf16),
        compiler_params=pltpu.CompilerParams(dimension_semantics=("arbitrary",),
                                             vmem_limit_bytes=VMEM_LIMIT),
        name="forget_cumsum",
    )(logf)


def _flash_sweep(qq, k_ref, vt_ref, aug_ref, acc_sc, m_sc, l_sc, i, t):
    n = qq.shape[0]
    m_sc[...] = jnp.full(m_sc.shape, -jnp.inf, f32)
    l_sc[...] = jnp.zeros(l_sc.shape, f32)
    acc_sc[...] = jnp.zeros(acc_sc.shape, f32)

    def scores(j):
        k = k_ref[0, 0, pl.ds(j * t, t), :]
        if aug_ref is not None:
            k = jnp.concatenate([k, aug_ref[0, 0, pl.ds(j * t, t), :]], axis=1)
        return _nt_dot(k, qq)

    def update(s, j, diag):
        if diag:
            key = lax.broadcasted_iota(jnp.int32, s.shape, 0)
            qry = lax.broadcasted_iota(jnp.int32, s.shape, 1)
            if n > t:
                qry = jnp.where(qry >= t, qry - t, qry)
            s = jnp.where(key <= qry, s, -jnp.inf)
        vt = vt_ref[0, 0, :, pl.ds(j * t, t)]
        m_prev = m_sc[...]
        m_new = jnp.maximum(m_prev, jnp.max(s, axis=0, keepdims=True))
        alpha = jnp.exp2(m_prev - m_new)
        p = jnp.exp2(s - m_new)
        l_sc[...] = alpha * l_sc[...] + jnp.sum(p, axis=0, keepdims=True)
        acc_sc[...] = alpha * acc_sc[...] + jnp.dot(vt, p.astype(bf16), preferred_element_type=f32)
        m_sc[...] = m_new

    def pair(j0, diag1):
        s0 = scores(j0)
        s1 = scores(j0 + 1)
        update(s0, j0, False)
        update(s1, j0 + 1, diag1)

    def body(pidx, carry):
        pair(2 * pidx, False)
        return carry

    lax.fori_loop(0, i // 2, body, 0)

    @pl.when(i % 2 == 1)
    def _():
        pair(i - 1, True)

    @pl.when(i % 2 == 0)
    def _():
        update(scores(i), i, True)


def _diff_attn_kernel(q_ref, k_ref, vt_ref, lam_ref, g_ref, o_ref, acc_sc, m_sc, l_sc, *,
                      tq, lam_init):
    i = pl.program_id(2)
    q = q_ref[0, 0]
    lane = lax.broadcasted_iota(jnp.int32, q.shape, 1)
    zero = jnp.zeros_like(q)
    qq = jnp.concatenate([jnp.where(lane < 64, q, zero), jnp.where(lane >= 64, q, zero)], axis=0)
    _flash_sweep(qq, k_ref, vt_ref, None, acc_sc, m_sc, l_sc, i, tq)
    lam = (jnp.exp(jnp.sum(lam_ref[0:1, :] * lam_ref[1:2, :], axis=1, keepdims=True))
           - jnp.exp(jnp.sum(lam_ref[2:3, :] * lam_ref[3:4, :], axis=1, keepdims=True)) + lam_init)
    ot = acc_sc[...] / l_sc[...]
    ot = ot[:, 0:tq] - lam * ot[:, tq:2 * tq]
    ms = jnp.mean(ot * ot, axis=0, keepdims=True)
    ot = ot * lax.rsqrt(ms + RMS_EPS) * g_ref[...] * (1.0 - lam_init)
    o_ref[0, 0] = ot.T.astype(bf16)


def _fox_attn_kernel(q_ref, k_ref, vt_ref, aug_ref, o_ref, acc_sc, m_sc, l_sc, *, tq):
    i = pl.program_id(2)
    q = q_ref[0, 0]
    lane = lax.broadcasted_iota(jnp.int32, q.shape, 1)
    ones3 = jnp.where(lane < 3, 1.0, 0.0).astype(bf16)
    qq = jnp.concatenate([q, ones3], axis=1)
    _flash_sweep(qq, k_ref, vt_ref, aug_ref, acc_sc, m_sc, l_sc, i, tq)
    o_ref[0, 0] = (acc_sc[...] / l_sc[...]).T.astype(bf16)


def _layer0_attention(q, k, vt, extra, extra_specs, kern, n_stack, tq, name):
    nq = SEQ // tq
    q_spec = pl.BlockSpec((1, 1, tq, LANES), lambda b, h, i: (b, h, i, 0))
    k_spec = pl.BlockSpec((1, 1, SEQ, LANES), lambda b, h, i: (b, h, 0, 0))
    vt_spec = pl.BlockSpec((1, 1, LANES, SEQ), lambda b, h, i: (b, h, 0, 0))
    n = n_stack * tq
    return pl.pallas_call(
        kern,
        grid=(BATCH, 4, nq),
        in_specs=[q_spec, k_spec, vt_spec] + extra_specs,
        out_specs=q_spec,
        out_shape=jax.ShapeDtypeStruct((BATCH, 4, SEQ, LANES), bf16),
        scratch_shapes=[pltpu.VMEM((LANES, n), f32), pltpu.VMEM((1, n), f32), pltpu.VMEM((1, n), f32)],
        compiler_params=pltpu.CompilerParams(
            dimension_semantics=("arbitrary", "arbitrary", "arbitrary"), vmem_limit_bytes=VMEM_LIMIT),
        name=name,
    )(q, k, vt, *extra)


def _pattern_bias(dil):
    n_c = N_RES // dil
    bq = max(128 // n_c, 16)
    bk = 2 * bq
    cq, iq = np.divmod(np.arange(n_c * bq), bq)
    ck, jk = np.divmod(np.arange(n_c * bk), bk)
    out = []
    for shift in (0, bq):
        dist = n_c * (shift + iq[:, None] - jk[None, :]) + (cq[:, None] - ck[None, :])
        out.append(np.where((dist >= 0) & (dist <= 128), 0.0, NEG))
    return jnp.asarray(np.stack(out), dtype=f32), n_c, bq


def _dilated_kernel(q_ref, k_ref, v_ref, b16_ref, b4_ref, b1_ref, o_ref, acc_sc, m_sc, l_sc, *, geo):
    def pieces(ref, lead, slabs, start, n):
        parts = [ref[lead + (s, pl.ds(start, n), slice(None))] for s in slabs]
        return parts[0] if len(parts) == 1 else jnp.concatenate(parts, axis=0)

    def load(slabs, ub, bq, bias_ref, first):
        u0 = ub * bq
        ks = jnp.maximum(ub - 1, 0) * bq
        t = dict(slabs=slabs, u0=u0, bq=bq, first=first,
                 q=pieces(q_ref, (0, 0), slabs, u0, bq),
                 k=pieces(k_ref, (0, 0), slabs, ks, 2 * bq),
                 v=pieces(v_ref, (0, 0), slabs, ks, 2 * bq),
                 bias=bias_ref[jnp.minimum(ub, 1)])
        if not first:
            t["m"] = pieces(m_sc, (), slabs, u0, bq)
            t["l"] = pieces(l_sc, (), slabs, u0, bq)
            t["a"] = pieces(acc_sc, (), slabs, u0, bq)
        return t

    def compute(t):
        s = _nt_dot(t["q"], t["k"]) + t["bias"]
        m_cur = jnp.max(s, axis=1, keepdims=True)
        v_ext = jnp.concatenate([t["v"], jnp.ones(t["v"].shape, bf16)], axis=1)
        if t["first"]:
            m_new = jnp.broadcast_to(m_cur, (s.shape[0], LANES))
        else:
            m_new = jnp.maximum(t["m"], m_cur)
            alpha = jnp.exp2(t["m"] - m_new)
        p = jnp.exp2(s - jnp.concatenate([m_new] * (s.shape[1] // LANES), axis=1))
        ext = jnp.dot(p.astype(bf16), v_ext, preferred_element_type=f32)
        a_new, l_new = ext[:, :LANES], ext[:, LANES:]
        if not t["first"]:
            l_new = l_new + alpha * t["l"]
            a_new = a_new + alpha * t["a"]
        return m_new, l_new, a_new

    def store(t, res):
        bq, u0 = t["bq"], t["u0"]
        for n, s_id in enumerate(t["slabs"]):
            rows = slice(n * bq, (n + 1) * bq)
            for ref, val in zip((m_sc, l_sc, acc_sc), res):
                ref[s_id, pl.ds(u0, bq), :] = val[rows]

    def run(tiles):
        results = [compute(t) for t in tiles]
        for t, res in zip(tiles, results):
            store(t, res)

    first = True
    for (dil, n_c, bq), bias_ref in zip(geo, (b16_ref, b4_ref, b1_ref)):
        n_tiles = SLAB // bq
        if n_c == 1:
            def body(rg, carry, bq=bq, bias_ref=bias_ref, first=first, n_tiles=n_tiles):
                run([load([UNROLL_16 * rg + dr], ub, bq, bias_ref, first)
                     for dr in range(UNROLL_16) for ub in range(n_tiles)])
                return carry
            lax.fori_loop(0, N_RES // UNROLL_16, body, 0)
        elif n_c == N_RES:
            def body(pp, carry, bq=bq, bias_ref=bias_ref, first=first):
                run([load(list(range(N_RES)), UNROLL_1 * pp + du, bq, bias_ref, first)
                     for du in range(UNROLL_1)])
                return carry
            lax.fori_loop(0, n_tiles // UNROLL_1, body, 0)
        else:
            def body(up, carry, dil=dil, n_c=n_c, bq=bq, bias_ref=bias_ref, first=first):
                run([load([c * dil + res for c in range(n_c)], UNROLL_4 * up + du, bq, bias_ref, first)
                     for du in range(UNROLL_4) for res in range(dil)])
                return carry
            lax.fori_loop(0, n_tiles // UNROLL_4, body, 0)
        first = False

    def fin(r, carry):
        o_ref[0, 0, r] = (acc_sc[r] / l_sc[r]).astype(bf16)
        return carry
    lax.fori_loop(0, N_RES, fin, 0)


def _dilated_attention(q, k, v):
    biases, geo = [], []
    for dil in (16, 4, 1):
        b, n_c, bq = _pattern_bias(dil)
        biases.append(b)
        geo.append((dil, n_c, bq))
    spec = pl.BlockSpec((1, 1, N_RES, SLAB, LANES), lambda b, h: (b, h, 0, 0, 0))
    return pl.pallas_call(
        functools.partial(_dilated_kernel, geo=tuple(geo)),
        grid=(BATCH, 8),
        in_specs=[spec, spec, spec] + [_const_spec(b.shape) for b in biases],
        out_specs=spec,
        out_shape=jax.ShapeDtypeStruct((BATCH, 8, N_RES, SLAB, LANES), bf16),
        scratch_shapes=[pltpu.VMEM((N_RES, SLAB, LANES), f32)] * 3,
        compiler_params=pltpu.CompilerParams(dimension_semantics=("arbitrary", "arbitrary"),
                                             vmem_limit_bytes=VMEM_LIMIT),
        name="dilated_attn",
    )(q, k, v, *biases)


def _post_kernel(h_ref, *rest, n_o, res_rows):
    o_refs = rest[:n_o]
    z_ref, p_ref, wo_ref, wg_ref, wp_ref, out_ref, mixed_sc, h1_sc = rest[n_o:]
    heads_per = D_MODEL // LANES // n_o
    for n, o_ref in enumerate(o_refs):
        for hh in range(heads_per):
            col = (n * heads_per + hh) * LANES
            if res_rows is None:
                o = o_ref[0, hh].astype(f32)
                z = z_ref[:, col:col + LANES].astype(f32)
                mixed_sc[:, col:col + LANES] = (o * (z * _sigmoid(z))).astype(bf16)
            else:
                for rr in range(8):
                    o = o_ref[0, hh, rr].astype(f32)
                    z = z_ref[0, rr, :, col:col + LANES].astype(f32)
                    mixed_sc[rr * res_rows:(rr + 1) * res_rows, col:col + LANES] = (
                        o * (z * _sigmoid(z))).astype(bf16)
    if res_rows is None:
        h, pb = h_ref[...], p_ref[...]
    else:
        h = jnp.concatenate([h_ref[0, :, rr, :] for rr in range(8)], axis=0)
        pb = jnp.concatenate([p_ref[0, :, rr, :] for rr in range(8)], axis=0)
    pb = pb.astype(bf16)
    h1 = h + jnp.dot(mixed_sc[...], wo_ref[...], preferred_element_type=f32)
    h1_sc[...] = h1
    h1b = h1.astype(bf16)
    for c in range(D_MODEL // CHUNK):
        cols = slice(c * CHUNK, (c + 1) * CHUNK)
        gate = _sigmoid(jnp.dot(h1b, wg_ref[:, cols], preferred_element_type=f32))
        emb = jnp.dot(pb, wp_ref[:, cols], preferred_element_type=f32)
        res = h1_sc[:, cols] + emb * gate
        if res_rows is None:
            out_ref[:, cols] = res
        else:
            for rr in range(8):
                out_ref[0, :, rr, cols] = res[rr * res_rows:(rr + 1) * res_rows]


def _post_even(x2d, oa, ob, z, p0, wo, wg, wp, tm):
    n_t = (BATCH * SEQ) // tm
    per_b = SEQ // tm
    row_spec = lambda w: pl.BlockSpec((tm, w), lambda i: (i, 0))
    head_spec = pl.BlockSpec((1, 4, tm, LANES), lambda i: (i // per_b, 0, i % per_b, 0))
    return pl.pallas_call(
        functools.partial(_post_kernel, n_o=2, res_rows=None),
        grid=(n_t,),
        in_specs=[row_spec(D_MODEL), head_spec, head_spec, row_spec(D_MODEL),
                  pl.BlockSpec((None, tm, PLE_DIM), lambda i: (0, i, 0)),
                  _const_spec(wo.shape), _const_spec(wg.shape), _const_spec(wp.shape)],
        out_specs=row_spec(D_MODEL),
        out_shape=jax.ShapeDtypeStruct((BATCH * SEQ, D_MODEL), f32),
        scratch_shapes=[pltpu.VMEM((tm, D_MODEL), bf16), pltpu.VMEM((tm, D_MODEL), f32)],
        compiler_params=pltpu.CompilerParams(dimension_semantics=("arbitrary",),
                                             vmem_limit_bytes=VMEM_LIMIT),
        name="post_even",
    )(x2d, oa, ob, z, p0, wo, wg, wp)


def _post_odd(h_res, o, z, p_res, wo, wg, wp):
    tm = 8 * RES_ROWS
    res_row = lambda w: pl.BlockSpec((1, RES_ROWS, 8, w), lambda b, g, c: (b, c, g, 0))
    return pl.pallas_call(
        functools.partial(_post_kernel, n_o=1, res_rows=RES_ROWS),
        grid=(BATCH, N_RES // 8, SLAB // RES_ROWS),
        in_specs=[res_row(D_MODEL),
                  pl.BlockSpec((1, 8, 8, RES_ROWS, LANES), lambda b, g, c: (b, 0, g, c, 0)),
                  pl.BlockSpec((1, 8, RES_ROWS, D_MODEL), lambda b, g, c: (b, g, c, 0)),
                  pl.BlockSpec((None, 1, RES_ROWS, 8, PLE_DIM), lambda b, g, c: (1, b, c, g, 0)),
                  _const_spec(wo.shape), _const_spec(wg.shape), _const_spec(wp.shape)],
        out_specs=res_row(D_MODEL),
        out_shape=jax.ShapeDtypeStruct((BATCH, SLAB, N_RES, D_MODEL), f32),
        scratch_shapes=[pltpu.VMEM((tm, D_MODEL), bf16), pltpu.VMEM((tm, D_MODEL), f32)],
        compiler_params=pltpu.CompilerParams(dimension_semantics=("arbitrary",) * 3,
                                             vmem_limit_bytes=VMEM_LIMIT),
        name="post_odd",
    )(h_res, o, z, p_res, wo, wg, wp)


def _tile4(g):
    return jnp.tile(g.astype(f32), CHUNK // g.shape[0])


def kernel(x, p, positions, norm_g, w_in_even, b_forget, qn_a, kn_a, qn_b, kn_b, lam_q1, lam_k1,
           lam_q2, lam_k2, subln_g, w_out_even, w_in_odd, qn_c, kn_c, w_out_odd, w_ple, w_ple_gate):
    tm = 512
    tq_diff, tq_fox = 512, 512
    n_tok = BATCH * SEQ
    x2d = x.reshape(n_tok, D_MODEL)
    pos_f = positions.astype(f32)

    w0 = w_in_even[0]
    f_lo, f_hi = 3072, 3076
    w_main0 = jnp.concatenate([w0[:, :f_lo], w0[:, f_hi:]], axis=1).astype(bf16)
    w_f = jnp.pad(w0[:, f_lo:f_hi], ((0, 0), (0, LANES - 4))).astype(bf16)
    b_f = jnp.pad(b_forget[0].astype(f32), (0, LANES - 4)).reshape(1, LANES)
    ones_row = jnp.ones((CHUNK,), f32)
    gains0 = jnp.stack([_tile4(qn_a[0]) * (64 ** -0.5 * LOG2E), _tile4(kn_a[0]), ones_row,
                        _tile4(qn_b[0]) * (HEAD_DIM ** -0.5 * LOG2E), _tile4(kn_b[0]), ones_row,
                        ones_row, ones_row])
    qa, ka, va, qb, kb, vb, z0, logf = _proj_even(
        x2d, pos_f.reshape(n_tok // tm, 1, tm), norm_g[0].reshape(1, D_MODEL).astype(f32), w_main0, gains0,
        w_f, b_f, tm)
    cum = _forget_cumsum(logf)

    lam_rows = jnp.pad(jnp.stack([lam_q1[0], lam_k1[0], lam_q2[0], lam_k2[0]]).astype(f32),
                       ((0, 4), (0, LANES - 64)))
    lam_init = 0.8 - 0.6 * math.exp(-0.3 * 0)
    oa = _layer0_attention(
        qa, ka, va, [lam_rows, subln_g[0].reshape(LANES, 1).astype(f32)],
        [_const_spec((8, LANES)), _const_spec((LANES, 1))],
        functools.partial(_diff_attn_kernel, tq=tq_diff, lam_init=lam_init),
        2, tq_diff, "diff_attn")
    ob = _layer0_attention(
        qb, kb, vb, [cum], [pl.BlockSpec((1, 1, SEQ, LANES), lambda b, h, i: (b, h, 0, 0))],
        functools.partial(_fox_attn_kernel, tq=tq_fox), 1, tq_fox, "fox_attn")
    h1 = _post_even(x2d, oa, ob, z0, p.reshape(2, n_tok, PLE_DIM), w_out_even[0].astype(bf16),
                    w_ple_gate[0].astype(bf16), w_ple[0].astype(bf16), tm)

    h_res = h1.reshape(BATCH, SLAB, N_RES, D_MODEL)
    pos_res = (pos_f.reshape(BATCH, SLAB // RES_ROWS, RES_ROWS, N_RES // 8, 8)
               .transpose(0, 3, 1, 4, 2).reshape(-1, 1, 8 * RES_ROWS))
    ones_row2 = jnp.ones((CHUNK,), f32)
    gq = _tile4(qn_c[0]) * (HEAD_DIM ** -0.5 * LOG2E)
    gk = _tile4(kn_c[0])
    gains1 = jnp.stack([gq, gq, gk, gk, ones_row2, ones_row2, ones_row2, ones_row2])
    q1, k1, v1, z1 = _proj_odd(h_res, pos_res, norm_g[1].reshape(1, D_MODEL).astype(f32),
                               w_in_odd[0].astype(bf16), gains1)
    o1 = _dilated_attention(q1, k1, v1)
    out = _post_odd(h_res, o1, z1, p.reshape(2, BATCH, SLAB, N_RES, PLE_DIM),
                    w_out_odd[0].astype(bf16), w_ple_gate[1].astype(bf16), w_ple[1].astype(bf16))
    return out.reshape(BATCH, SEQ, D_MODEL)
```

```python
import functools
import math

import numpy as np
import jax
import jax.numpy as jnp
from jax import lax
from jax.experimental import pallas as pl
from jax.experimental.pallas import tpu as pltpu

D_MODEL = 1024
BATCH = 8
SEQ = 4096
PLE_DIM = 256
HEAD_DIM = 128
ROPE_THETA = 500000.0
RMS_EPS = 1e-6
LANES = 128
CHUNK = 512
N_RES = 16
SLAB = SEQ // N_RES
RES_ROWS = 64
VMEM_LIMIT = 56 * 1024 * 1024
NEG = -1e30
LOG2E = math.log2(math.e)
FLASH_GROUP = 4
SUM_ROWS = 16
UNROLL_16, UNROLL_4, UNROLL_1 = 4, 2, 4

bf16 = jnp.bfloat16
f32 = jnp.float32


def _nt_dot(a, b):
    return lax.dot_general(a, b, (((1,), (1,)), ((), ())), preferred_element_type=f32)


def _sigmoid(x):
    return 1.0 / (1.0 + jnp.exp(-x))


def _proj_kernel(x_ref, pos_ref, ng_ref, w_ref, gains_ref, ones_ref, invf_ref,
                 *rest, chunks, rope_half, rope_group, has_f, res_rows):
    if has_f:
        wf_ref, bf_ref = rest[0], rest[1]
        out_refs = rest[2:]
    else:
        out_refs = rest
    if res_rows is None:
        x = x_ref[...]
    else:
        x = jnp.concatenate([x_ref[0, :, rr, :] for rr in range(8)], axis=0)
    ms = jnp.mean(x * x, axis=-1, keepdims=True)
    hn = (x * lax.rsqrt(ms + RMS_EPS) * ng_ref[...]).astype(bf16)

    tm = x.shape[0]
    ang = invf_ref[...] * pos_ref[...].reshape(1, tm)
    cos_c, sin_c = jnp.cos(ang), jnp.sin(ang)
    zeros_c = jnp.zeros_like(ang)
    rest_rows = rope_group - 2 * rope_half
    reps = LANES // rope_group

    def lane_table(first, second, fill):
        blk = [first, second, jnp.full((rest_rows, tm), fill, f32)]
        return jnp.concatenate(blk * reps, axis=0).T

    cosv = lane_table(cos_c, cos_c, 1.0)
    sin_hi = lane_table(zeros_c, sin_c, 0.0)
    sin_lo = lane_table(-sin_c, zeros_c, 0.0)

    def put_head(o_ref, head, yb):
        if res_rows is None:
            o_ref[0, head] = yb
        else:
            for rr in range(8):
                o_ref[0, head, rr] = yb[rr * res_rows:(rr + 1) * res_rows]

    def put_cols(o_ref, col, yb):
        if res_rows is None:
            o_ref[:, col:col + LANES] = yb
        else:
            for rr in range(8):
                o_ref[0, rr, :, col:col + LANES] = yb[rr * res_rows:(rr + 1) * res_rows]

    for c, (kind, group, oi, slot) in enumerate(chunks):
        acc = jnp.dot(hn, w_ref[:, c * CHUNK:(c + 1) * CHUNK], preferred_element_type=f32)
        if kind in ("rope", "norm") and group < LANES:
            ss = jnp.dot((acc * acc).astype(bf16), ones_ref[...], preferred_element_type=f32)
            acc = acc * lax.rsqrt(ss * (1.0 / group) + RMS_EPS) * gains_ref[c:c + 1, :]
        o_ref = out_refs[oi]
        for hh in range(CHUNK // LANES):
            yb = acc[:, hh * LANES:(hh + 1) * LANES]
            if kind in ("rope", "norm") and group == LANES:
                ms = jnp.mean(yb * yb, axis=-1, keepdims=True)
                yb = yb * lax.rsqrt(ms + RMS_EPS) * gains_ref[c:c + 1, hh * LANES:(hh + 1) * LANES]
            if kind == "rope":
                yb = (yb * cosv + pltpu.roll(yb, rope_half, 1) * sin_hi
                      + pltpu.roll(yb, LANES - rope_half, 1) * sin_lo)
            if kind == "plain_t":
                o_ref[0, slot * (CHUNK // LANES) + hh] = yb.T.astype(bf16)
            elif kind == "z":
                put_cols(o_ref, slot * CHUNK + hh * LANES, yb.astype(bf16))
            else:
                put_head(o_ref, slot * (CHUNK // LANES) + hh, yb.astype(bf16))

    if has_f:
        f = jnp.dot(hn, wf_ref[...], preferred_element_type=f32) + bf_ref[...]
        out_refs[-1][...] = jnp.minimum(f, 0.0) - jnp.log1p(jnp.exp(-jnp.abs(f)))


def _inv_freq_col(half):
    return jnp.exp(-math.log(ROPE_THETA) * jnp.arange(half, dtype=f32) / half).reshape(half, 1)


def _block_ones(group):
    i = np.arange(CHUNK)
    return jnp.asarray((i[:, None] // group == i[None, :] // group).astype(np.float32), dtype=bf16)


def _const_spec(shape):
    nd = len(shape)
    return pl.BlockSpec(shape, lambda *_: (0,) * nd)


def _proj_even(x2d, pos_col, ng, w_main, gains, w_f, b_f, tm):
    n_t = (BATCH * SEQ) // tm
    per_b = SEQ // tm
    chunks = (("rope", 64, 0, 0), ("rope", 64, 1, 0), ("plain_t", 0, 2, 0),
              ("norm", 128, 3, 0), ("norm", 128, 4, 0), ("plain_t", 0, 5, 0),
              ("z", 0, 6, 0), ("z", 0, 6, 1))
    head_shape = jax.ShapeDtypeStruct((BATCH, 4, SEQ, LANES), bf16)
    head_spec = pl.BlockSpec((1, 4, tm, LANES), lambda i: (i // per_b, 0, i % per_b, 0))
    head_t_shape = jax.ShapeDtypeStruct((BATCH, 4, LANES, SEQ), bf16)
    head_t_spec = pl.BlockSpec((1, 4, LANES, tm), lambda i: (i // per_b, 0, 0, i % per_b))
    row_spec = lambda w: pl.BlockSpec((tm, w), lambda i: (i, 0))
    kern = functools.partial(_proj_kernel, chunks=chunks, rope_half=8, rope_group=64, has_f=True,
                             res_rows=None)
    return pl.pallas_call(
        kern,
        grid=(n_t,),
        in_specs=[row_spec(D_MODEL), pl.BlockSpec((1, 1, tm), lambda i: (i, 0, 0)),
                  _const_spec((1, D_MODEL)),
                  _const_spec(w_main.shape), _const_spec((8, CHUNK)),
                  _const_spec((CHUNK, CHUNK)),
                  _const_spec((8, 1)), _const_spec((D_MODEL, LANES)), _const_spec((1, LANES))],
        out_specs=[head_spec, head_spec, head_t_spec] * 2 + [row_spec(D_MODEL), row_spec(LANES)],
        out_shape=[head_shape, head_shape, head_t_shape] * 2 + [
            jax.ShapeDtypeStruct((BATCH * SEQ, D_MODEL), bf16),
            jax.ShapeDtypeStruct((BATCH * SEQ, LANES), f32)],
        compiler_params=pltpu.CompilerParams(dimension_semantics=("arbitrary",),
                                             vmem_limit_bytes=VMEM_LIMIT),
        name="proj_even",
    )(x2d, pos_col, ng, w_main, gains, _block_ones(64), _inv_freq_col(8),
      w_f, b_f)


def _proj_odd(h_res, pos_res, ng, w_main, gains):
    chunks = (("rope", 128, 0, 0), ("rope", 128, 0, 1), ("rope", 128, 1, 0), ("rope", 128, 1, 1),
              ("plain", 0, 2, 0), ("plain", 0, 2, 1), ("z", 0, 3, 0), ("z", 0, 3, 1))
    head_shape = jax.ShapeDtypeStruct((BATCH, 8, N_RES, SLAB, LANES), bf16)
    head_spec = pl.BlockSpec((1, 8, 8, RES_ROWS, LANES), lambda b, g, c: (b, 0, g, c, 0))
    kern = functools.partial(_proj_kernel, chunks=chunks, rope_half=16, rope_group=128, has_f=False,
                             res_rows=RES_ROWS)
    n_g, n_c = N_RES // 8, SLAB // RES_ROWS
    return pl.pallas_call(
        kern,
        grid=(BATCH, n_g, n_c),
        in_specs=[pl.BlockSpec((1, RES_ROWS, 8, D_MODEL), lambda b, g, c: (b, c, g, 0)),
                  pl.BlockSpec((1, 1, 8 * RES_ROWS), lambda b, g, c: ((b * n_g + g) * n_c + c, 0, 0)),
                  _const_spec((1, D_MODEL)), _const_spec(w_main.shape), _const_spec((8, CHUNK)),
                  _const_spec((CHUNK, CHUNK)),
                  _const_spec((16, 1))],
        out_specs=[head_spec] * 3 + [pl.BlockSpec((1, 8, RES_ROWS, D_MODEL), lambda b, g, c: (b, g, c, 0))],
        out_shape=[head_shape] * 3 + [jax.ShapeDtypeStruct((BATCH, N_RES, SLAB, D_MODEL), bf16)],
        compiler_params=pltpu.CompilerParams(dimension_semantics=("arbitrary",) * 3,
                                             vmem_limit_bytes=VMEM_LIMIT),
        name="proj_odd",
    )(h_res, pos_res, ng, w_main, gains, _block_ones(64), _inv_freq_col(16))


def _split3(x):
    hi = x.astype(bf16).astype(f32)
    r1 = x - hi
    mid = r1.astype(bf16).astype(f32)
    lo = (r1 - mid).astype(bf16).astype(f32)
    return hi, mid, lo


def _lane_place(lane, base, terms, init):
    out = init
    for n, t in enumerate(terms):
        out = jnp.where(lane == base + n, t, out)
    return out


def _cumsum_kernel(logf_ref, aug_ref, *, blk):
    row = lax.broadcasted_iota(jnp.int32, (blk, blk), 0)
    col = lax.broadcasted_iota(jnp.int32, (blk, blk), 1)
    lower = (col <= row).astype(bf16)
    lane = lax.broadcasted_iota(jnp.int32, (blk, LANES), 1)
    carry = jnp.zeros((1, LANES), f32)
    for i in range(SEQ // blk):
        x = logf_ref[i * blk:(i + 1) * blk, :]
        packed = jnp.zeros((blk, LANES), f32)
        for hh in range(4):
            xh = jnp.broadcast_to(x[:, hh:hh + 1], (blk, LANES))
            packed = _lane_place(lane, 3 * hh, _split3(xh), packed)
        cs = jnp.dot(lower, packed.astype(bf16), preferred_element_type=f32) + carry
        carry = cs[blk - 1:blk, :]
        for hh in range(4):
            c = (cs[:, 3 * hh:3 * hh + 1] + cs[:, 3 * hh + 1:3 * hh + 2]) + cs[:, 3 * hh + 2:3 * hh + 3]
            cb = jnp.broadcast_to(c * (-LOG2E), (blk, LANES))
            aug_ref[0, hh, i * blk:(i + 1) * blk, :] = _lane_place(
                lane, 0, _split3(cb), jnp.zeros((blk, LANES), f32)).astype(bf16)


def _forget_cumsum(logf):
    return pl.pallas_call(
        functools.partial(_cumsum_kernel, blk=512),
        grid=(BATCH,),
        in_specs=[pl.BlockSpec((SEQ, LANES), lambda b: (b, 0))],
        out_specs=pl.BlockSpec((1, 4, SEQ, LANES), lambda b: (b, 0, 0, 0)),
        out_shape=jax.ShapeDtypeStruct((BATCH, 4, SEQ, LANES), bf16),
        compiler_params=pltpu.CompilerParams(dimension_semantics=("arbitrary",),
                                             vmem_limit_bytes=VMEM_LIMIT),
        name="forget_cumsum",
    )(logf)


def _flash_sweep(qq, k_ref, vt_ref, aug_ref, acc_sc, m_sc, i, t):
    n = qq.shape[0]
    m_sc[...] = jnp.full(m_sc.shape, -jnp.inf, f32)
    acc_sc[...] = jnp.zeros(acc_sc.shape, f32)

    def scores(j):
        k = k_ref[0, 0, pl.ds(j * t, t), :]
        if aug_ref is not None:
            k = jnp.concatenate([k, aug_ref[0, 0, pl.ds(j * t, t), :]], axis=1)
        return _nt_dot(k, qq)

    def update(s, j, diag):
        if diag:
            key = lax.broadcasted_iota(jnp.int32, s.shape, 0)
            qry = lax.broadcasted_iota(jnp.int32, s.shape, 1)
            if n > t:
                qry = jnp.where(qry >= t, qry - t, qry)
            s = jnp.where(key <= qry, s, -jnp.inf)
        vt = vt_ref[0, 0, :, pl.ds(j * t, t)]
        vt_ext = jnp.concatenate([vt, jnp.ones((SUM_ROWS, t), bf16)], axis=0)
        m_prev = m_sc[...]
        m_new = jnp.maximum(m_prev, jnp.max(s, axis=0, keepdims=True))
        alpha = jnp.exp2(m_prev - m_new)
        p = jnp.exp2(s - m_new).astype(bf16)
        acc_sc[...] = alpha * acc_sc[...] + jnp.dot(vt_ext, p, preferred_element_type=f32)
        m_sc[...] = m_new

    def group(j0, size, last_diag):
        ss = [scores(j0 + d) for d in range(size)]
        for d in range(size):
            update(ss[d], j0 + d, last_diag and d == size - 1)

    def body(g, carry):
        group(FLASH_GROUP * g, FLASH_GROUP, False)
        return carry

    lax.fori_loop(0, i // FLASH_GROUP, body, 0)
    for rem in range(FLASH_GROUP):
        @pl.when(i % FLASH_GROUP == rem)
        def _(rem=rem):
            group(i - rem, rem + 1, True)


def _diff_attn_kernel(q_ref, k_ref, vt_ref, lam_ref, g_ref, o_ref, acc_sc, m_sc, *,
                      tq, lam_init):
    i = pl.program_id(2)
    q = q_ref[0, 0]
    lane = lax.broadcasted_iota(jnp.int32, q.shape, 1)
    zero = jnp.zeros_like(q)
    qq = jnp.concatenate([jnp.where(lane < 64, q, zero), jnp.where(lane >= 64, q, zero)], axis=0)
    _flash_sweep(qq, k_ref, vt_ref, None, acc_sc, m_sc, i, tq)
    lam = (jnp.exp(jnp.sum(lam_ref[0:1, :] * lam_ref[1:2, :], axis=1, keepdims=True))
           - jnp.exp(jnp.sum(lam_ref[2:3, :] * lam_ref[3:4, :], axis=1, keepdims=True)) + lam_init)
    ot = acc_sc[0:LANES, :] / acc_sc[LANES:LANES + 1, :]
    ot = ot[:, 0:tq] - lam * ot[:, tq:2 * tq]
    ms = jnp.mean(ot * ot, axis=0, keepdims=True)
    ot = ot * lax.rsqrt(ms + RMS_EPS) * g_ref[...] * (1.0 - lam_init)
    o_ref[0, 0] = ot.T.astype(bf16)


def _fox_attn_kernel(q_ref, k_ref, vt_ref, aug_ref, o_ref, acc_sc, m_sc, *, tq):
    i = pl.program_id(2)
    q = q_ref[0, 0]
    lane = lax.broadcasted_iota(jnp.int32, q.shape, 1)
    ones3 = jnp.where(lane < 3, 1.0, 0.0).astype(bf16)
    qq = jnp.concatenate([q, ones3], axis=1)
    _flash_sweep(qq, k_ref, vt_ref, aug_ref, acc_sc, m_sc, i, tq)
    o_ref[0, 0] = (acc_sc[0:LANES, :] / acc_sc[LANES:LANES + 1, :]).T.astype(bf16)


def _layer0_attention(q, k, vt, extra, extra_specs, kern, n_stack, tq, name):
    nq = SEQ // tq
    q_spec = pl.BlockSpec((1, 1, tq, LANES), lambda b, h, i: (b, h, i, 0))
    k_spec = pl.BlockSpec((1, 1, SEQ, LANES), lambda b, h, i: (b, h, 0, 0))
    vt_spec = pl.BlockSpec((1, 1, LANES, SEQ), lambda b, h, i: (b, h, 0, 0))
    n = n_stack * tq
    return pl.pallas_call(
        kern,
        grid=(BATCH, 4, nq),
        in_specs=[q_spec, k_spec, vt_spec] + extra_specs,
        out_specs=q_spec,
        out_shape=jax.ShapeDtypeStruct((BATCH, 4, SEQ, LANES), bf16),
        scratch_shapes=[pltpu.VMEM((LANES + SUM_ROWS, n), f32), pltpu.VMEM((1, n), f32)],
        compiler_params=pltpu.CompilerParams(
            dimension_semantics=("arbitrary", "arbitrary", "arbitrary"), vmem_limit_bytes=VMEM_LIMIT),
        name=name,
    )(q, k, vt, *extra)


def _pattern_bias(dil):
    n_c = N_RES // dil
    bq = max(128 // n_c, 16)
    bk = 2 * bq
    cq, iq = np.divmod(np.arange(n_c * bq), bq)
    ck, jk = np.divmod(np.arange(n_c * bk), bk)
    out = []
    for shift in (0, bq):
        dist = n_c * (shift + iq[:, None] - jk[None, :]) + (cq[:, None] - ck[None, :])
        out.append(np.where((dist >= 0) & (dist <= 128), 0.0, NEG))
    return jnp.asarray(np.stack(out), dtype=f32), n_c, bq


def _dilated_kernel(q_ref, k_ref, v_ref, b16_ref, b4_ref, b1_ref, o_ref, acc_sc, m_sc, l_sc, *, geo):
    def pieces(ref, lead, slabs, start, n):
        parts = [ref[lead + (s, pl.ds(start, n), slice(None))] for s in slabs]
        return parts[0] if len(parts) == 1 else jnp.concatenate(parts, axis=0)

    def load(slabs, ub, bq, bias_ref, first):
        u0 = ub * bq
        ks = jnp.maximum(ub - 1, 0) * bq
        t = dict(slabs=slabs, u0=u0, bq=bq, first=first,
                 q=pieces(q_ref, (0, 0), slabs, u0, bq),
                 k=pieces(k_ref, (0, 0), slabs, ks, 2 * bq),
                 v=pieces(v_ref, (0, 0), slabs, ks, 2 * bq),
                 bias=bias_ref[jnp.minimum(ub, 1)])
        if not first:
            t["m"] = pieces(m_sc, (), slabs, u0, bq)
            t["l"] = pieces(l_sc, (), slabs, u0, bq)
            t["a"] = pieces(acc_sc, (), slabs, u0, bq)
        return t

    def compute(t):
        s = _nt_dot(t["q"], t["k"]) + t["bias"]
        m_cur = jnp.max(s, axis=1, keepdims=True)
        v_ext = jnp.concatenate([t["v"], jnp.ones(t["v"].shape, bf16)], axis=1)
        if t["first"]:
            m_new = jnp.broadcast_to(m_cur, (s.shape[0], LANES))
        else:
            m_new = jnp.maximum(t["m"], m_cur)
            alpha = jnp.exp2(t["m"] - m_new)
        p = jnp.exp2(s - jnp.concatenate([m_new] * (s.shape[1] // LANES), axis=1))
        ext = jnp.dot(p.astype(bf16), v_ext, preferred_element_type=f32)
        a_new, l_new = ext[:, :LANES], ext[:, LANES:]
        if not t["first"]:
            l_new = l_new + alpha * t["l"]
            a_new = a_new + alpha * t["a"]
        return m_new, l_new, a_new

    def store(t, res):
        bq, u0 = t["bq"], t["u0"]
        for n, s_id in enumerate(t["slabs"]):
            rows = slice(n * bq, (n + 1) * bq)
            for ref, val in zip((m_sc, l_sc, acc_sc), res):
                ref[s_id, pl.ds(u0, bq), :] = val[rows]

    def run(tiles):
        results = [compute(t) for t in tiles]
        for t, res in zip(tiles, results):
            store(t, res)

    first = True
    for (dil, n_c, bq), bias_ref in zip(geo, (b16_ref, b4_ref, b1_ref)):
        n_tiles = SLAB // bq
        if n_c == 1:
            def body(rg, carry, bq=bq, bias_ref=bias_ref, first=first, n_tiles=n_tiles):
                run([load([UNROLL_16 * rg + dr], ub, bq, bias_ref, first)
                     for dr in range(UNROLL_16) for ub in range(n_tiles)])
                return carry
            lax.fori_loop(0, N_RES // UNROLL_16, body, 0)
        elif n_c == N_RES:
            def body(pp, carry, bq=bq, bias_ref=bias_ref, first=first):
                run([load(list(range(N_RES)), UNROLL_1 * pp + du, bq, bias_ref, first)
                     for du in range(UNROLL_1)])
                return carry
            lax.fori_loop(0, n_tiles // UNROLL_1, body, 0)
        else:
            def body(up, carry, dil=dil, n_c=n_c, bq=bq, bias_ref=bias_ref, first=first):
                run([load([c * dil + res for c in range(n_c)], UNROLL_4 * up + du, bq, bias_ref, first)
                     for du in range(UNROLL_4) for res in range(dil)])
                return carry
            lax.fori_loop(0, n_tiles // UNROLL_4, body, 0)
        first = False

    def fin(r, carry):
        o_ref[0, 0, r] = (acc_sc[r] / l_sc[r]).astype(bf16)
        return carry
    lax.fori_loop(0, N_RES, fin, 0)


def _dilated_attention(q, k, v):
    biases, geo = [], []
    for dil in (16, 4, 1):
        b, n_c, bq = _pattern_bias(dil)
        biases.append(b)
        geo.append((dil, n_c, bq))
    spec = pl.BlockSpec((1, 1, N_RES, SLAB, LANES), lambda b, h: (b, h, 0, 0, 0))
    return pl.pallas_call(
        functools.partial(_dilated_kernel, geo=tuple(geo)),
        grid=(BATCH, 8),
        in_specs=[spec, spec, spec] + [_const_spec(b.shape) for b in biases],
        out_specs=spec,
        out_shape=jax.ShapeDtypeStruct((BATCH, 8, N_RES, SLAB, LANES), bf16),
        scratch_shapes=[pltpu.VMEM((N_RES, SLAB, LANES), f32)] * 3,
        compiler_params=pltpu.CompilerParams(dimension_semantics=("arbitrary", "arbitrary"),
                                             vmem_limit_bytes=VMEM_LIMIT),
        name="dilated_attn",
    )(q, k, v, *biases)


def _post_kernel(h_ref, *rest, n_o, res_rows):
    o_refs = rest[:n_o]
    z_ref, p_ref, wo_ref, wg_ref, wp_ref, out_ref, mixed_sc, h1_sc = rest[n_o:]
    heads_per = D_MODEL // LANES // n_o
    for n, o_ref in enumerate(o_refs):
        for hh in range(heads_per):
            col = (n * heads_per + hh) * LANES
            if res_rows is None:
                o = o_ref[0, hh].astype(f32)
                z = z_ref[:, col:col + LANES].astype(f32)
                mixed_sc[:, col:col + LANES] = (o * (z * _sigmoid(z))).astype(bf16)
            else:
                for rr in range(8):
                    o = o_ref[0, hh, rr].astype(f32)
                    z = z_ref[0, rr, :, col:col + LANES].astype(f32)
                    mixed_sc[rr * res_rows:(rr + 1) * res_rows, col:col + LANES] = (
                        o * (z * _sigmoid(z))).astype(bf16)
    if res_rows is None:
        h, pb = h_ref[...], p_ref[...]
    else:
        h = jnp.concatenate([h_ref[0, :, rr, :] for rr in range(8)], axis=0)
        pb = jnp.concatenate([p_ref[0, :, rr, :] for rr in range(8)], axis=0)
    pb = pb.astype(bf16)
    h1 = h + jnp.dot(mixed_sc[...], wo_ref[...], preferred_element_type=f32)
    h1_sc[...] = h1
    h1b = h1.astype(bf16)
    for c in range(D_MODEL // CHUNK):
        cols = slice(c * CHUNK, (c + 1) * CHUNK)
        gate = _sigmoid(jnp.dot(h1b, wg_ref[:, cols], preferred_element_type=f32))
        emb = jnp.dot(pb, wp_ref[:, cols], preferred_element_type=f32)
        res = h1_sc[:, cols] + emb * gate
        if res_rows is None:
            out_ref[:, cols] = res
        else:
            for rr in range(8):
                out_ref[0, :, rr, cols] = res[rr * res_rows:(rr + 1) * res_rows]


def _post_even(x2d, oa, ob, z, p0, wo, wg, wp, tm):
    n_t = (BATCH * SEQ) // tm
    per_b = SEQ // tm
    row_spec = lambda w: pl.BlockSpec((tm, w), lambda i: (i, 0))
    head_spec = pl.BlockSpec((1, 4, tm, LANES), lambda i: (i // per_b, 0, i % per_b, 0))
    return pl.pallas_call(
        functools.partial(_post_kernel, n_o=2, res_rows=None),
        grid=(n_t,),
        in_specs=[row_spec(D_MODEL), head_spec, head_spec, row_spec(D_MODEL),
                  pl.BlockSpec((None, tm, PLE_DIM), lambda i: (0, i, 0)),
                  _const_spec(wo.shape), _const_spec(wg.shape), _const_spec(wp.shape)],
        out_specs=row_spec(D_MODEL),
        out_shape=jax.ShapeDtypeStruct((BATCH * SEQ, D_MODEL), f32),
        scratch_shapes=[pltpu.VMEM((tm, D_MODEL), bf16), pltpu.VMEM((tm, D_MODEL), f32)],
        compiler_params=pltpu.CompilerParams(dimension_semantics=("arbitrary",),
                                             vmem_limit_bytes=VMEM_LIMIT),
        name="post_even",
    )(x2d, oa, ob, z, p0, wo, wg, wp)


def _post_odd(h_res, o, z, p_res, wo, wg, wp):
    tm = 8 * RES_ROWS
    res_row = lambda w: pl.BlockSpec((1, RES_ROWS, 8, w), lambda b, g, c: (b, c, g, 0))
    return pl.pallas_call(
        functools.partial(_post_kernel, n_o=1, res_rows=RES_ROWS),
        grid=(BATCH, N_RES // 8, SLAB // RES_ROWS),
        in_specs=[res_row(D_MODEL),
                  pl.BlockSpec((1, 8, 8, RES_ROWS, LANES), lambda b, g, c: (b, 0, g, c, 0)),
                  pl.BlockSpec((1, 8, RES_ROWS, D_MODEL), lambda b, g, c: (b, g, c, 0)),
                  pl.BlockSpec((None, 1, RES_ROWS, 8, PLE_DIM), lambda b, g, c: (1, b, c, g, 0)),
                  _const_spec(wo.shape), _const_spec(wg.shape), _const_spec(wp.shape)],
        out_specs=res_row(D_MODEL),
        out_shape=jax.ShapeDtypeStruct((BATCH, SLAB, N_RES, D_MODEL), f32),
        scratch_shapes=[pltpu.VMEM((tm, D_MODEL), bf16), pltpu.VMEM((tm, D_MODEL), f32)],
        compiler_params=pltpu.CompilerParams(dimension_semantics=("arbitrary",) * 3,
                                             vmem_limit_bytes=VMEM_LIMIT),
        name="post_odd",
    )(h_res, o, z, p_res, wo, wg, wp)


def _tile4(g):
    return jnp.tile(g.astype(f32), CHUNK // g.shape[0])


def kernel(x, p, positions, norm_g, w_in_even, b_forget, qn_a, kn_a, qn_b, kn_b, lam_q1, lam_k1,
           lam_q2, lam_k2, subln_g, w_out_even, w_in_odd, qn_c, kn_c, w_out_odd, w_ple, w_ple_gate):
    tm = 512
    tq_diff, tq_fox = 512, 512
    n_tok = BATCH * SEQ
    x2d = x.reshape(n_tok, D_MODEL)
    pos_f = positions.astype(f32)

    w0 = w_in_even[0]
    f_lo, f_hi = 3072, 3076
    w_main0 = jnp.concatenate([w0[:, :f_lo], w0[:, f_hi:]], axis=1).astype(bf16)
    w_f = jnp.pad(w0[:, f_lo:f_hi], ((0, 0), (0, LANES - 4))).astype(bf16)
    b_f = jnp.pad(b_forget[0].astype(f32), (0, LANES - 4)).reshape(1, LANES)
    ones_row = jnp.ones((CHUNK,), f32)
    gains0 = jnp.stack([_tile4(qn_a[0]) * (64 ** -0.5 * LOG2E), _tile4(kn_a[0]), ones_row,
                        _tile4(qn_b[0]) * (HEAD_DIM ** -0.5 * LOG2E), _tile4(kn_b[0]), ones_row,
                        ones_row, ones_row])
    qa, ka, va, qb, kb, vb, z0, logf = _proj_even(
        x2d, pos_f.reshape(n_tok // tm, 1, tm), norm_g[0].reshape(1, D_MODEL).astype(f32), w_main0, gains0,
        w_f, b_f, tm)
    cum = _forget_cumsum(logf)

    lam_rows = jnp.pad(jnp.stack([lam_q1[0], lam_k1[0], lam_q2[0], lam_k2[0]]).astype(f32),
                       ((0, 4), (0, LANES - 64)))
    lam_init = 0.8 - 0.6 * math.exp(-0.3 * 0)
    oa = _layer0_attention(
        qa, ka, va, [lam_rows, subln_g[0].reshape(LANES, 1).astype(f32)],
        [_const_spec((8, LANES)), _const_spec((LANES, 1))],
        functools.partial(_diff_attn_kernel, tq=tq_diff, lam_init=lam_init),
        2, tq_diff, "diff_attn")
    ob = _layer0_attention(
        qb, kb, vb, [cum], [pl.BlockSpec((1, 1, SEQ, LANES), lambda b, h, i: (b, h, 0, 0))],
        functools.partial(_fox_attn_kernel, tq=tq_fox), 1, tq_fox, "fox_attn")
    h1 = _post_even(x2d, oa, ob, z0, p.reshape(2, n_tok, PLE_DIM), w_out_even[0].astype(bf16),
                    w_ple_gate[0].astype(bf16), w_ple[0].astype(bf16), tm)

    h_res = h1.reshape(BATCH, SLAB, N_RES, D_MODEL)
    pos_res = (pos_f.reshape(BATCH, SLAB // RES_ROWS, RES_ROWS, N_RES // 8, 8)
               .transpose(0, 3, 1, 4, 2).reshape(-1, 1, 8 * RES_ROWS))
    ones_row2 = jnp.ones((CHUNK,), f32)
    gq = _tile4(qn_c[0]) * (HEAD_DIM ** -0.5 * LOG2E)
    gk = _tile4(kn_c[0])
    gains1 = jnp.stack([gq, gq, gk, gk, ones_row2, ones_row2, ones_row2, ones_row2])
    q1, k1, v1, z1 = _proj_odd(h_res, pos_res, norm_g[1].reshape(1, D_MODEL).astype(f32),
                               w_in_odd[0].astype(bf16), gains1)
    o1 = _dilated_attention(q1, k1, v1)
    out = _post_odd(h_res, o1, z1, p.reshape(2, BATCH, SLAB, N_RES, PLE_DIM),
                    w_out_odd[0].astype(bf16), w_ple_gate[1].astype(bf16), w_ple[1].astype(bf16))
    return out.reshape(BATCH, SEQ, D_MODEL)
```

```python
import functools
import math

import numpy as np
import jax
import jax.numpy as jnp
from jax import lax
from jax.experimental import pallas as pl
from jax.experimental.pallas import tpu as pltpu

D_MODEL = 1024
BATCH = 8
SEQ = 4096
PLE_DIM = 256
HEAD_DIM = 128
ROPE_THETA = 500000.0
RMS_EPS = 1e-6
LANES = 128
CHUNK = 512
N_RES = 16
SLAB = SEQ // N_RES
RES_ROWS = 64
VMEM_LIMIT = 56 * 1024 * 1024
NEG = -1e30
LOG2E = math.log2(math.e)
FLASH_GROUP = 4
SUM_ROWS = 16
UNROLL_16, UNROLL_4, UNROLL_1 = 8, 4, 8

bf16 = jnp.bfloat16
f32 = jnp.float32


def _nt_dot(a, b):
    return lax.dot_general(a, b, (((1,), (1,)), ((), ())), preferred_element_type=f32)


def _sigmoid(x):
    return 1.0 / (1.0 + jnp.exp(-x))


def _proj_kernel(x_ref, pos_ref, ng_ref, w_ref, gains_ref, ones_ref, invf_ref,
                 *rest, chunks, rope_half, rope_group, has_f, res_rows):
    if has_f:
        wf_ref, bf_ref = rest[0], rest[1]
        out_refs = rest[2:]
    else:
        out_refs = rest
    if res_rows is None:
        x = x_ref[...]
    else:
        x = jnp.concatenate([x_ref[0, :, rr, :] for rr in range(8)], axis=0)
    ms = jnp.mean(x * x, axis=-1, keepdims=True)
    hn = (x * lax.rsqrt(ms + RMS_EPS) * ng_ref[...]).astype(bf16)

    tm = x.shape[0]
    ang = invf_ref[...] * pos_ref[...].reshape(1, tm)
    cos_c, sin_c = jnp.cos(ang), jnp.sin(ang)
    zeros_c = jnp.zeros_like(ang)
    rest_rows = rope_group - 2 * rope_half
    reps = LANES // rope_group

    def lane_table(first, second, fill):
        blk = [first, second, jnp.full((rest_rows, tm), fill, f32)]
        return jnp.concatenate(blk * reps, axis=0).T

    cosv = lane_table(cos_c, cos_c, 1.0)
    sin_hi = lane_table(zeros_c, sin_c, 0.0)
    sin_lo = lane_table(-sin_c, zeros_c, 0.0)

    def put_head(o_ref, head, yb):
        if res_rows is None:
            o_ref[0, head] = yb
        else:
            for rr in range(8):
                o_ref[0, head, rr] = yb[rr * res_rows:(rr + 1) * res_rows]

    def put_cols(o_ref, col, yb):
        if res_rows is None:
            o_ref[:, col:col + LANES] = yb
        else:
            for rr in range(8):
                o_ref[0, rr, :, col:col + LANES] = yb[rr * res_rows:(rr + 1) * res_rows]

    for c, (kind, group, oi, slot) in enumerate(chunks):
        acc = jnp.dot(hn, w_ref[:, c * CHUNK:(c + 1) * CHUNK], preferred_element_type=f32)
        if kind in ("rope", "norm") and group < LANES:
            ss = jnp.dot((acc * acc).astype(bf16), ones_ref[...], preferred_element_type=f32)
            acc = acc * lax.rsqrt(ss * (1.0 / group) + RMS_EPS) * gains_ref[c:c + 1, :]
        o_ref = out_refs[oi]
        for hh in range(CHUNK // LANES):
            yb = acc[:, hh * LANES:(hh + 1) * LANES]
            if kind in ("rope", "norm") and group == LANES:
                ms = jnp.mean(yb * yb, axis=-1, keepdims=True)
                yb = yb * lax.rsqrt(ms + RMS_EPS) * gains_ref[c:c + 1, hh * LANES:(hh + 1) * LANES]
            if kind == "rope":
                yb = (yb * cosv + pltpu.roll(yb, rope_half, 1) * sin_hi
                      + pltpu.roll(yb, LANES - rope_half, 1) * sin_lo)
            if kind == "plain_t":
                o_ref[0, slot * (CHUNK // LANES) + hh] = yb.T.astype(bf16)
            elif kind == "z":
                put_cols(o_ref, slot * CHUNK + hh * LANES, yb.astype(bf16))
            else:
                put_head(o_ref, slot * (CHUNK // LANES) + hh, yb.astype(bf16))

    if has_f:
        f = jnp.dot(hn, wf_ref[...], preferred_element_type=f32) + bf_ref[...]
        out_refs[-1][...] = jnp.minimum(f, 0.0) - jnp.log1p(jnp.exp(-jnp.abs(f)))


def _inv_freq_col(half):
    return jnp.exp(-math.log(ROPE_THETA) * jnp.arange(half, dtype=f32) / half).reshape(half, 1)


def _block_ones(group):
    i = np.arange(CHUNK)
    return jnp.asarray((i[:, None] // group == i[None, :] // group).astype(np.float32), dtype=bf16)


def _const_spec(shape):
    nd = len(shape)
    return pl.BlockSpec(shape, lambda *_: (0,) * nd)


def _proj_even(x2d, pos_col, ng, w_main, gains, w_f, b_f, tm):
    n_t = (BATCH * SEQ) // tm
    per_b = SEQ // tm
    chunks = (("rope", 64, 0, 0), ("rope", 64, 1, 0), ("plain_t", 0, 2, 0),
              ("norm", 128, 3, 0), ("norm", 128, 4, 0), ("plain_t", 0, 5, 0),
              ("z", 0, 6, 0), ("z", 0, 6, 1))
    head_shape = jax.ShapeDtypeStruct((BATCH, 4, SEQ, LANES), bf16)
    head_spec = pl.BlockSpec((1, 4, tm, LANES), lambda i: (i // per_b, 0, i % per_b, 0))
    head_t_shape = jax.ShapeDtypeStruct((BATCH, 4, LANES, SEQ), bf16)
    head_t_spec = pl.BlockSpec((1, 4, LANES, tm), lambda i: (i // per_b, 0, 0, i % per_b))
    row_spec = lambda w: pl.BlockSpec((tm, w), lambda i: (i, 0))
    kern = functools.partial(_proj_kernel, chunks=chunks, rope_half=8, rope_group=64, has_f=True,
                             res_rows=None)
    return pl.pallas_call(
        kern,
        grid=(n_t,),
        in_specs=[row_spec(D_MODEL), pl.BlockSpec((1, 1, tm), lambda i: (i, 0, 0)),
                  _const_spec((1, D_MODEL)),
                  _const_spec(w_main.shape), _const_spec((8, CHUNK)),
                  _const_spec((CHUNK, CHUNK)),
                  _const_spec((8, 1)), _const_spec((D_MODEL, LANES)), _const_spec((1, LANES))],
        out_specs=[head_spec, head_spec, head_t_spec] * 2 + [row_spec(D_MODEL), row_spec(LANES)],
        out_shape=[head_shape, head_shape, head_t_shape] * 2 + [
            jax.ShapeDtypeStruct((BATCH * SEQ, D_MODEL), bf16),
            jax.ShapeDtypeStruct((BATCH * SEQ, LANES), f32)],
        compiler_params=pltpu.CompilerParams(dimension_semantics=("arbitrary",),
                                             vmem_limit_bytes=VMEM_LIMIT),
        name="proj_even",
    )(x2d, pos_col, ng, w_main, gains, _block_ones(64), _inv_freq_col(8),
      w_f, b_f)


def _proj_odd(h_res, pos_res, ng, w_main, gains):
    chunks = (("rope", 128, 0, 0), ("rope", 128, 0, 1), ("rope", 128, 1, 0), ("rope", 128, 1, 1),
              ("plain", 0, 2, 0), ("plain", 0, 2, 1), ("z", 0, 3, 0), ("z", 0, 3, 1))
    head_shape = jax.ShapeDtypeStruct((BATCH, 8, N_RES, SLAB, LANES), bf16)
    head_spec = pl.BlockSpec((1, 8, 8, RES_ROWS, LANES), lambda b, g, c: (b, 0, g, c, 0))
    kern = functools.partial(_proj_kernel, chunks=chunks, rope_half=16, rope_group=128, has_f=False,
                             res_rows=RES_ROWS)
    n_g, n_c = N_RES // 8, SLAB // RES_ROWS
    return pl.pallas_call(
        kern,
        grid=(BATCH, n_g, n_c),
        in_specs=[pl.BlockSpec((1, RES_ROWS, 8, D_MODEL), lambda b, g, c: (b, c, g, 0)),
                  pl.BlockSpec((1, 1, 8 * RES_ROWS), lambda b, g, c: ((b * n_g + g) * n_c + c, 0, 0)),
                  _const_spec((1, D_MODEL)), _const_spec(w_main.shape), _const_spec((8, CHUNK)),
                  _const_spec((CHUNK, CHUNK)),
                  _const_spec((16, 1))],
        out_specs=[head_spec] * 3 + [pl.BlockSpec((1, 8, RES_ROWS, D_MODEL), lambda b, g, c: (b, g, c, 0))],
        out_shape=[head_shape] * 3 + [jax.ShapeDtypeStruct((BATCH, N_RES, SLAB, D_MODEL), bf16)],
        compiler_params=pltpu.CompilerParams(dimension_semantics=("arbitrary",) * 3,
                                             vmem_limit_bytes=VMEM_LIMIT),
        name="proj_odd",
    )(h_res, pos_res, ng, w_main, gains, _block_ones(64), _inv_freq_col(16))


def _split3(x):
    hi = x.astype(bf16).astype(f32)
    r1 = x - hi
    mid = r1.astype(bf16).astype(f32)
    lo = (r1 - mid).astype(bf16).astype(f32)
    return hi, mid, lo


def _split3_lanes(x):
    return jnp.concatenate(_split3(x), axis=1).astype(bf16)


def _cumsum_kernel(logf_ref, place_ref, aug_ref, *, blk):
    row = lax.broadcasted_iota(jnp.int32, (blk, blk), 0)
    col = lax.broadcasted_iota(jnp.int32, (blk, blk), 1)
    lower = (col <= row).astype(bf16)
    carry = jnp.zeros((1, LANES), f32)
    for i in range(SEQ // blk):
        x = logf_ref[i * blk:(i + 1) * blk, :]
        cs3 = jnp.dot(lower, _split3_lanes(x), preferred_element_type=f32)
        cs = (cs3[:, 0:LANES] + cs3[:, LANES:2 * LANES]) + cs3[:, 2 * LANES:3 * LANES] + carry
        carry = cs[blk - 1:blk, :]
        placed = jnp.dot(_split3_lanes(cs * (-LOG2E)), place_ref[...], preferred_element_type=f32)
        for hh in range(4):
            aug_ref[0, hh, i * blk:(i + 1) * blk, :] = placed[:, hh * LANES:(hh + 1) * LANES].astype(bf16)


def _place_matrices():
    pm = np.zeros((3 * LANES, 4 * LANES), np.float32)
    for k in range(3):
        for hh in range(4):
            pm[k * LANES + hh, hh * LANES + k] = 1.0
    return jnp.asarray(pm, dtype=bf16)


def _forget_cumsum(logf):
    return pl.pallas_call(
        functools.partial(_cumsum_kernel, blk=512),
        grid=(BATCH,),
        in_specs=[pl.BlockSpec((SEQ, LANES), lambda b: (b, 0)), _const_spec((3 * LANES, 4 * LANES))],
        out_specs=pl.BlockSpec((1, 4, SEQ, LANES), lambda b: (b, 0, 0, 0)),
        out_shape=jax.ShapeDtypeStruct((BATCH, 4, SEQ, LANES), bf16),
        compiler_params=pltpu.CompilerParams(dimension_semantics=("arbitrary",),
                                             vmem_limit_bytes=VMEM_LIMIT),
        name="forget_cumsum",
    )(logf, _place_matrices())


def _flash_sweep(qq, k_ref, vt_ref, aug_ref, acc_sc, m_sc, i, t):
    n = qq.shape[0]
    m_sc[...] = jnp.full(m_sc.shape, -jnp.inf, f32)
    acc_sc[...] = jnp.zeros(acc_sc.shape, f32)

    def scores(j):
        k = k_ref[0, 0, pl.ds(j * t, t), :]
        if aug_ref is not None:
            k = jnp.concatenate([k, aug_ref[0, 0, pl.ds(j * t, t), :]], axis=1)
        return _nt_dot(k, qq)

    def update(s, j, diag):
        if diag:
            key = lax.broadcasted_iota(jnp.int32, s.shape, 0)
            qry = lax.broadcasted_iota(jnp.int32, s.shape, 1)
            if n > t:
                qry = jnp.where(qry >= t, qry - t, qry)
            s = jnp.where(key <= qry, s, -jnp.inf)
        vt = vt_ref[0, 0, :, pl.ds(j * t, t)]
        vt_ext = jnp.concatenate([vt, jnp.ones((SUM_ROWS, t), bf16)], axis=0)
        m_prev = m_sc[...]
        m_new = jnp.maximum(m_prev, jnp.max(s, axis=0, keepdims=True))
        alpha = jnp.exp2(m_prev - m_new)
        p = jnp.exp2(s - m_new).astype(bf16)
        acc_sc[...] = alpha * acc_sc[...] + jnp.dot(vt_ext, p, preferred_element_type=f32)
        m_sc[...] = m_new

    def group(j0, size, last_diag):
        ss = [scores(j0 + d) for d in range(size)]
        for d in range(size):
            update(ss[d], j0 + d, last_diag and d == size - 1)

    def body(g, carry):
        group(FLASH_GROUP * g, FLASH_GROUP, False)
        return carry

    lax.fori_loop(0, i // FLASH_GROUP, body, 0)
    for rem in range(FLASH_GROUP):
        @pl.when(i % FLASH_GROUP == rem)
        def _(rem=rem):
            group(i - rem, rem + 1, True)


def _diff_attn_kernel(q_ref, k_ref, vt_ref, lam_ref, g_ref, o_ref, acc_sc, m_sc, *,
                      tq, lam_init):
    i = pl.program_id(2)
    q = q_ref[0, 0]
    lane = lax.broadcasted_iota(jnp.int32, q.shape, 1)
    zero = jnp.zeros_like(q)
    qq = jnp.concatenate([jnp.where(lane < 64, q, zero), jnp.where(lane >= 64, q, zero)], axis=0)
    _flash_sweep(qq, k_ref, vt_ref, None, acc_sc, m_sc, i, tq)
    lam = (jnp.exp(jnp.sum(lam_ref[0:1, :] * lam_ref[1:2, :], axis=1, keepdims=True))
           - jnp.exp(jnp.sum(lam_ref[2:3, :] * lam_ref[3:4, :], axis=1, keepdims=True)) + lam_init)
    ot = acc_sc[0:LANES, :] / acc_sc[LANES:LANES + 1, :]
    ot = ot[:, 0:tq] - lam * ot[:, tq:2 * tq]
    ms = jnp.mean(ot * ot, axis=0, keepdims=True)
    ot = ot * lax.rsqrt(ms + RMS_EPS) * g_ref[...] * (1.0 - lam_init)
    o_ref[0, 0] = ot.T.astype(bf16)


def _fox_attn_kernel(q_ref, k_ref, vt_ref, aug_ref, o_ref, acc_sc, m_sc, *, tq):
    i = pl.program_id(2)
    q = q_ref[0, 0]
    lane = lax.broadcasted_iota(jnp.int32, q.shape, 1)
    ones3 = jnp.where(lane < 3, 1.0, 0.0).astype(bf16)
    qq = jnp.concatenate([q, ones3], axis=1)
    _flash_sweep(qq, k_ref, vt_ref, aug_ref, acc_sc, m_sc, i, tq)
    o_ref[0, 0] = (acc_sc[0:LANES, :] / acc_sc[LANES:LANES + 1, :]).T.astype(bf16)


def _layer0_attention(q, k, vt, extra, extra_specs, kern, n_stack, tq, name):
    nq = SEQ // tq
    q_spec = pl.BlockSpec((1, 1, tq, LANES), lambda b, h, i: (b, h, i, 0))
    k_spec = pl.BlockSpec((1, 1, SEQ, LANES), lambda b, h, i: (b, h, 0, 0))
    vt_spec = pl.BlockSpec((1, 1, LANES, SEQ), lambda b, h, i: (b, h, 0, 0))
    n = n_stack * tq
    return pl.pallas_call(
        kern,
        grid=(BATCH, 4, nq),
        in_specs=[q_spec, k_spec, vt_spec] + extra_specs,
        out_specs=q_spec,
        out_shape=jax.ShapeDtypeStruct((BATCH, 4, SEQ, LANES), bf16),
        scratch_shapes=[pltpu.VMEM((LANES + SUM_ROWS, n), f32), pltpu.VMEM((1, n), f32)],
        compiler_params=pltpu.CompilerParams(
            dimension_semantics=("arbitrary", "arbitrary", "arbitrary"), vmem_limit_bytes=VMEM_LIMIT),
        name=name,
    )(q, k, vt, *extra)


def _pattern_bias(dil):
    n_c = N_RES // dil
    bq = max(128 // n_c, 16)
    bk = 2 * bq
    cq, iq = np.divmod(np.arange(n_c * bq), bq)
    ck, jk = np.divmod(np.arange(n_c * bk), bk)
    out = []
    for shift in (0, bq):
        dist = n_c * (shift + iq[:, None] - jk[None, :]) + (cq[:, None] - ck[None, :])
        out.append(np.where((dist >= 0) & (dist <= 128), 0.0, NEG))
    return jnp.asarray(np.stack(out), dtype=f32), n_c, bq


def _dilated_kernel(q_ref, k_ref, v_ref, b16_ref, b4_ref, b1_ref, o_ref, acc_sc, m_sc, l_sc, *, geo):
    def pieces(ref, lead, slabs, start, n):
        parts = [ref[lead + (s, pl.ds(start, n), slice(None))] for s in slabs]
        return parts[0] if len(parts) == 1 else jnp.concatenate(parts, axis=0)

    def load(slabs, ub, bq, bias_ref, first):
        u0 = ub * bq
        ks = jnp.maximum(ub - 1, 0) * bq
        t = dict(slabs=slabs, u0=u0, bq=bq, first=first,
                 q=pieces(q_ref, (0, 0), slabs, u0, bq),
                 k=pieces(k_ref, (0, 0), slabs, ks, 2 * bq),
                 v=pieces(v_ref, (0, 0), slabs, ks, 2 * bq),
                 bias=bias_ref[jnp.minimum(ub, 1)])
        if not first:
            t["m"] = pieces(m_sc, (), slabs, u0, bq)
            t["l"] = pieces(l_sc, (), slabs, u0, bq)
            t["a"] = pieces(acc_sc, (), slabs, u0, bq)
        return t

    def compute(t):
        s = _nt_dot(t["q"], t["k"]) + t["bias"]
        m_cur = jnp.max(s, axis=1, keepdims=True)
        v_ext = jnp.concatenate([t["v"], jnp.ones(t["v"].shape, bf16)], axis=1)
        if t["first"]:
            m_new = jnp.broadcast_to(m_cur, (s.shape[0], LANES))
        else:
            m_new = jnp.maximum(t["m"], m_cur)
            alpha = jnp.exp2(t["m"] - m_new)
        p = jnp.exp2(s - jnp.concatenate([m_new] * (s.shape[1] // LANES), axis=1))
        ext = jnp.dot(p.astype(bf16), v_ext, preferred_element_type=f32)
        a_new, l_new = ext[:, :LANES], ext[:, LANES:]
        if not t["first"]:
            l_new = l_new + alpha * t["l"]
            a_new = a_new + alpha * t["a"]
        return m_new, l_new, a_new

    def store(t, res):
        bq, u0 = t["bq"], t["u0"]
        for n, s_id in enumerate(t["slabs"]):
            rows = slice(n * bq, (n + 1) * bq)
            for ref, val in zip((m_sc, l_sc, acc_sc), res):
                ref[s_id, pl.ds(u0, bq), :] = val[rows]

    def run(tiles):
        results = [compute(t) for t in tiles]
        for t, res in zip(tiles, results):
            store(t, res)

    first = True
    for (dil, n_c, bq), bias_ref in zip(geo, (b16_ref, b4_ref, b1_ref)):
        n_tiles = SLAB // bq
        if n_c == 1:
            def body(rg, carry, bq=bq, bias_ref=bias_ref, first=first, n_tiles=n_tiles):
                run([load([UNROLL_16 * rg + dr], ub, bq, bias_ref, first)
                     for dr in range(UNROLL_16) for ub in range(n_tiles)])
                return carry
            lax.fori_loop(0, N_RES // UNROLL_16, body, 0)
        elif n_c == N_RES:
            def body(pp, carry, bq=bq, bias_ref=bias_ref, first=first):
                run([load(list(range(N_RES)), UNROLL_1 * pp + du, bq, bias_ref, first)
                     for du in range(UNROLL_1)])
                return carry
            lax.fori_loop(0, n_tiles // UNROLL_1, body, 0)
        else:
            def body(up, carry, dil=dil, n_c=n_c, bq=bq, bias_ref=bias_ref, first=first):
                run([load([c * dil + res for c in range(n_c)], UNROLL_4 * up + du, bq, bias_ref, first)
                     for du in range(UNROLL_4) for res in range(dil)])
                return carry
            lax.fori_loop(0, n_tiles // UNROLL_4, body, 0)
        first = False

    def fin(r, carry):
        o_ref[0, 0, r] = (acc_sc[r] / l_sc[r]).astype(bf16)
        return carry
    lax.fori_loop(0, N_RES, fin, 0)


def _dilated_attention(q, k, v):
    biases, geo = [], []
    for dil in (16, 4, 1):
        b, n_c, bq = _pattern_bias(dil)
        biases.append(b)
        geo.append((dil, n_c, bq))
    spec = pl.BlockSpec((1, 1, N_RES, SLAB, LANES), lambda b, h: (b, h, 0, 0, 0))
    return pl.pallas_call(
        functools.partial(_dilated_kernel, geo=tuple(geo)),
        grid=(BATCH, 8),
        in_specs=[spec, spec, spec] + [_const_spec(b.shape) for b in biases],
        out_specs=spec,
        out_shape=jax.ShapeDtypeStruct((BATCH, 8, N_RES, SLAB, LANES), bf16),
        scratch_shapes=[pltpu.VMEM((N_RES, SLAB, LANES), f32)] * 3,
        compiler_params=pltpu.CompilerParams(dimension_semantics=("arbitrary", "arbitrary"),
                                             vmem_limit_bytes=VMEM_LIMIT),
        name="dilated_attn",
    )(q, k, v, *biases)


def _post_kernel(h_ref, *rest, n_o, res_rows):
    o_refs = rest[:n_o]
    z_ref, p_ref, wo_ref, wg_ref, wp_ref, out_ref, mixed_sc, h1_sc = rest[n_o:]
    heads_per = D_MODEL // LANES // n_o
    for n, o_ref in enumerate(o_refs):
        for hh in range(heads_per):
            col = (n * heads_per + hh) * LANES
            if res_rows is None:
                o = o_ref[0, hh].astype(f32)
                z = z_ref[:, col:col + LANES].astype(f32)
                mixed_sc[:, col:col + LANES] = (o * (z * _sigmoid(z))).astype(bf16)
            else:
                for rr in range(8):
                    o = o_ref[0, hh, rr].astype(f32)
                    z = z_ref[0, rr, :, col:col + LANES].astype(f32)
                    mixed_sc[rr * res_rows:(rr + 1) * res_rows, col:col + LANES] = (
                        o * (z * _sigmoid(z))).astype(bf16)
    if res_rows is None:
        h, pb = h_ref[...], p_ref[...]
    else:
        h = jnp.concatenate([h_ref[0, :, rr, :] for rr in range(8)], axis=0)
        pb = jnp.concatenate([p_ref[0, :, rr, :] for rr in range(8)], axis=0)
    pb = pb.astype(bf16)
    h1 = h + jnp.dot(mixed_sc[...], wo_ref[...], preferred_element_type=f32)
    h1_sc[...] = h1
    h1b = h1.astype(bf16)
    for c in range(D_MODEL // CHUNK):
        cols = slice(c * CHUNK, (c + 1) * CHUNK)
        gate = _sigmoid(jnp.dot(h1b, wg_ref[:, cols], preferred_element_type=f32))
        emb = jnp.dot(pb, wp_ref[:, cols], preferred_element_type=f32)
        res = h1_sc[:, cols] + emb * gate
        if res_rows is None:
            out_ref[:, cols] = res
        else:
            for rr in range(8):
                out_ref[0, :, rr, cols] = res[rr * res_rows:(rr + 1) * res_rows]


def _post_even(x2d, oa, ob, z, p0, wo, wg, wp, tm):
    n_t = (BATCH * SEQ) // tm
    per_b = SEQ // tm
    row_spec = lambda w: pl.BlockSpec((tm, w), lambda i: (i, 0))
    head_spec = pl.BlockSpec((1, 4, tm, LANES), lambda i: (i // per_b, 0, i % per_b, 0))
    return pl.pallas_call(
        functools.partial(_post_kernel, n_o=2, res_rows=None),
        grid=(n_t,),
        in_specs=[row_spec(D_MODEL), head_spec, head_spec, row_spec(D_MODEL),
                  pl.BlockSpec((None, tm, PLE_DIM), lambda i: (0, i, 0)),
                  _const_spec(wo.shape), _const_spec(wg.shape), _const_spec(wp.shape)],
        out_specs=row_spec(D_MODEL),
        out_shape=jax.ShapeDtypeStruct((BATCH * SEQ, D_MODEL), f32),
        scratch_shapes=[pltpu.VMEM((tm, D_MODEL), bf16), pltpu.VMEM((tm, D_MODEL), f32)],
        compiler_params=pltpu.CompilerParams(dimension_semantics=("arbitrary",),
                                             vmem_limit_bytes=VMEM_LIMIT),
        name="post_even",
    )(x2d, oa, ob, z, p0, wo, wg, wp)


def _post_odd(h_res, o, z, p_res, wo, wg, wp):
    tm = 8 * RES_ROWS
    res_row = lambda w: pl.BlockSpec((1, RES_ROWS, 8, w), lambda b, g, c: (b, c, g, 0))
    return pl.pallas_call(
        functools.partial(_post_kernel, n_o=1, res_rows=RES_ROWS),
        grid=(BATCH, N_RES // 8, SLAB // RES_ROWS),
        in_specs=[res_row(D_MODEL),
                  pl.BlockSpec((1, 8, 8, RES_ROWS, LANES), lambda b, g, c: (b, 0, g, c, 0)),
                  pl.BlockSpec((1, 8, RES_ROWS, D_MODEL), lambda b, g, c: (b, g, c, 0)),
                  pl.BlockSpec((None, 1, RES_ROWS, 8, PLE_DIM), lambda b, g, c: (1, b, c, g, 0)),
                  _const_spec(wo.shape), _const_spec(wg.shape), _const_spec(wp.shape)],
        out_specs=res_row(D_MODEL),
        out_shape=jax.ShapeDtypeStruct((BATCH, SLAB, N_RES, D_MODEL), f32),
        scratch_shapes=[pltpu.VMEM((tm, D_MODEL), bf16), pltpu.VMEM((tm, D_MODEL), f32)],
        compiler_params=pltpu.CompilerParams(dimension_semantics=("arbitrary",) * 3,
                                             vmem_limit_bytes=VMEM_LIMIT),
        name="post_odd",
    )(h_res, o, z, p_res, wo, wg, wp)


def _tile4(g):
    return jnp.tile(g.astype(f32), CHUNK // g.shape[0])


def kernel(x, p, positions, norm_g, w_in_even, b_forget, qn_a, kn_a, qn_b, kn_b, lam_q1, lam_k1,
           lam_q2, lam_k2, subln_g, w_out_even, w_in_odd, qn_c, kn_c, w_out_odd, w_ple, w_ple_gate):
    tm = 512
    tq_diff, tq_fox = 512, 512
    n_tok = BATCH * SEQ
    x2d = x.reshape(n_tok, D_MODEL)
    pos_f = positions.astype(f32)

    w0 = w_in_even[0]
    f_lo, f_hi = 3072, 3076
    w_main0 = jnp.concatenate([w0[:, :f_lo], w0[:, f_hi:]], axis=1).astype(bf16)
    w_f = jnp.pad(w0[:, f_lo:f_hi], ((0, 0), (0, LANES - 4))).astype(bf16)
    b_f = jnp.pad(b_forget[0].astype(f32), (0, LANES - 4)).reshape(1, LANES)
    ones_row = jnp.ones((CHUNK,), f32)
    gains0 = jnp.stack([_tile4(qn_a[0]) * (64 ** -0.5 * LOG2E), _tile4(kn_a[0]), ones_row,
                        _tile4(qn_b[0]) * (HEAD_DIM ** -0.5 * LOG2E), _tile4(kn_b[0]), ones_row,
                        ones_row, ones_row])
    qa, ka, va, qb, kb, vb, z0, logf = _proj_even(
        x2d, pos_f.reshape(n_tok // tm, 1, tm), norm_g[0].reshape(1, D_MODEL).astype(f32), w_main0, gains0,
        w_f, b_f, tm)
    cum = _forget_cumsum(logf)

    lam_rows = jnp.pad(jnp.stack([lam_q1[0], lam_k1[0], lam_q2[0], lam_k2[0]]).astype(f32),
                       ((0, 4), (0, LANES - 64)))
    lam_init = 0.8 - 0.6 * math.exp(-0.3 * 0)
    oa = _layer0_attention(
        qa, ka, va, [lam_rows, subln_g[0].reshape(LANES, 1).astype(f32)],
        [_const_spec((8, LANES)), _const_spec((LANES, 1))],
        functools.partial(_diff_attn_kernel, tq=tq_diff, lam_init=lam_init),
        2, tq_diff, "diff_attn")
    ob = _layer0_attention(
        qb, kb, vb, [cum], [pl.BlockSpec((1, 1, SEQ, LANES), lambda b, h, i: (b, h, 0, 0))],
        functools.partial(_fox_attn_kernel, tq=tq_fox), 1, tq_fox, "fox_attn")
    h1 = _post_even(x2d, oa, ob, z0, p.reshape(2, n_tok, PLE_DIM), w_out_even[0].astype(bf16),
                    w_ple_gate[0].astype(bf16), w_ple[0].astype(bf16), tm)

    h_res = h1.reshape(BATCH, SLAB, N_RES, D_MODEL)
    pos_res = (pos_f.reshape(BATCH, SLAB // RES_ROWS, RES_ROWS, N_RES // 8, 8)
               .transpose(0, 3, 1, 4, 2).reshape(-1, 1, 8 * RES_ROWS))
    ones_row2 = jnp.ones((CHUNK,), f32)
    gq = _tile4(qn_c[0]) * (HEAD_DIM ** -0.5 * LOG2E)
    gk = _tile4(kn_c[0])
    gains1 = jnp.stack([gq, gq, gk, gk, ones_row2, ones_row2, ones_row2, ones_row2])
    q1, k1, v1, z1 = _proj_odd(h_res, pos_res, norm_g[1].reshape(1, D_MODEL).astype(f32),
                               w_in_odd[0].astype(bf16), gains1)
    o1 = _dilated_attention(q1, k1, v1)
    out = _post_odd(h_res, o1, z1, p.reshape(2, BATCH, SLAB, N_RES, PLE_DIM),
                    w_out_odd[0].astype(bf16), w_ple_gate[1].astype(bf16), w_ple[1].astype(bf16))
    return out.reshape(BATCH, SEQ, D_MODEL)
```

```python
import functools
import math

import numpy as np
import jax
import jax.numpy as jnp
from jax import lax
from jax.experimental import pallas as pl
from jax.experimental.pallas import tpu as pltpu

D_MODEL = 1024
BATCH = 8
SEQ = 4096
PLE_DIM = 256
HEAD_DIM = 128
ROPE_THETA = 500000.0
RMS_EPS = 1e-6
LANES = 128
CHUNK = 512
N_RES = 16
SLAB = SEQ // N_RES
RES_ROWS = 64
VMEM_LIMIT = 56 * 1024 * 1024
NEG = -1e30
LOG2E = math.log2(math.e)
FLASH_GROUP = 4
SUM_ROWS = 16
UNROLL_16, UNROLL_4, UNROLL_1 = 8, 4, 8

bf16 = jnp.bfloat16
f32 = jnp.float32


def _nt_dot(a, b):
    return lax.dot_general(a, b, (((1,), (1,)), ((), ())), preferred_element_type=f32)


def _sigmoid(x):
    return 1.0 / (1.0 + jnp.exp(-x))


def _proj_kernel(x_ref, pos_ref, ng_ref, w_ref, gains_ref, ones_ref, invf_ref,
                 *rest, chunks, rope_half, rope_group, has_f, res_rows):
    if has_f:
        wf_ref, bf_ref = rest[0], rest[1]
        out_refs = rest[2:]
    else:
        out_refs = rest
    if res_rows is None:
        x = x_ref[...]
    else:
        x = jnp.concatenate([x_ref[0, :, rr, :] for rr in range(8)], axis=0)
    ms = jnp.mean(x * x, axis=-1, keepdims=True)
    hn = (x * lax.rsqrt(ms + RMS_EPS) * ng_ref[...]).astype(bf16)

    tm = x.shape[0]
    ang = invf_ref[...] * pos_ref[...].reshape(1, tm)
    cos_c, sin_c = jnp.cos(ang), jnp.sin(ang)
    zeros_c = jnp.zeros_like(ang)
    rest_rows = rope_group - 2 * rope_half
    reps = LANES // rope_group

    def lane_table(first, second, fill):
        blk = [first, second, jnp.full((rest_rows, tm), fill, f32)]
        return jnp.concatenate(blk * reps, axis=0).T

    cosv = lane_table(cos_c, cos_c, 1.0)
    sin_hi = lane_table(zeros_c, sin_c, 0.0)
    sin_lo = lane_table(-sin_c, zeros_c, 0.0)

    def put_head(o_ref, head, yb):
        if res_rows is None:
            o_ref[0, head] = yb
        else:
            for rr in range(8):
                o_ref[0, head, rr] = yb[rr * res_rows:(rr + 1) * res_rows]

    def put_cols(o_ref, col, yb):
        if res_rows is None:
            o_ref[:, col:col + LANES] = yb
        else:
            for rr in range(8):
                o_ref[0, rr, :, col:col + LANES] = yb[rr * res_rows:(rr + 1) * res_rows]

    for c, (kind, group, oi, slot) in enumerate(chunks):
        acc = jnp.dot(hn, w_ref[:, c * CHUNK:(c + 1) * CHUNK], preferred_element_type=f32)
        if kind in ("rope", "norm") and group < LANES:
            ss = jnp.dot((acc * acc).astype(bf16), ones_ref[...], preferred_element_type=f32)
            acc = acc * lax.rsqrt(ss * (1.0 / group) + RMS_EPS) * gains_ref[c:c + 1, :]
        o_ref = out_refs[oi]
        for hh in range(CHUNK // LANES):
            yb = acc[:, hh * LANES:(hh + 1) * LANES]
            if kind in ("rope", "norm") and group == LANES:
                ms = jnp.mean(yb * yb, axis=-1, keepdims=True)
                yb = yb * lax.rsqrt(ms + RMS_EPS) * gains_ref[c:c + 1, hh * LANES:(hh + 1) * LANES]
            if kind == "rope":
                yb = (yb * cosv + pltpu.roll(yb, rope_half, 1) * sin_hi
                      + pltpu.roll(yb, LANES - rope_half, 1) * sin_lo)
            if kind == "plain_t":
                o_ref[0, slot * (CHUNK // LANES) + hh] = yb.T.astype(bf16)
            elif kind == "z":
                put_cols(o_ref, slot * CHUNK + hh * LANES, yb.astype(bf16))
            else:
                put_head(o_ref, slot * (CHUNK // LANES) + hh, yb.astype(bf16))

    if has_f:
        f = jnp.dot(hn, wf_ref[...], preferred_element_type=f32) + bf_ref[...]
        out_refs[-1][...] = jnp.minimum(f, 0.0) - jnp.log1p(jnp.exp(-jnp.abs(f)))


def _inv_freq_col(half):
    return jnp.exp(-math.log(ROPE_THETA) * jnp.arange(half, dtype=f32) / half).reshape(half, 1)


def _block_ones(group):
    i = np.arange(CHUNK)
    return jnp.asarray((i[:, None] // group == i[None, :] // group).astype(np.float32), dtype=bf16)


def _const_spec(shape):
    nd = len(shape)
    return pl.BlockSpec(shape, lambda *_: (0,) * nd)


def _proj_even(x2d, pos_col, ng, w_main, gains, w_f, b_f, tm):
    n_t = (BATCH * SEQ) // tm
    per_b = SEQ // tm
    chunks = (("rope", 64, 0, 0), ("rope", 64, 1, 0), ("plain_t", 0, 2, 0),
              ("norm", 128, 3, 0), ("norm", 128, 4, 0), ("plain_t", 0, 5, 0),
              ("z", 0, 6, 0), ("z", 0, 6, 1))
    head_shape = jax.ShapeDtypeStruct((BATCH, 4, SEQ, LANES), bf16)
    head_spec = pl.BlockSpec((1, 4, tm, LANES), lambda i: (i // per_b, 0, i % per_b, 0))
    head_t_shape = jax.ShapeDtypeStruct((BATCH, 4, LANES, SEQ), bf16)
    head_t_spec = pl.BlockSpec((1, 4, LANES, tm), lambda i: (i // per_b, 0, 0, i % per_b))
    row_spec = lambda w: pl.BlockSpec((tm, w), lambda i: (i, 0))
    kern = functools.partial(_proj_kernel, chunks=chunks, rope_half=8, rope_group=64, has_f=True,
                             res_rows=None)
    return pl.pallas_call(
        kern,
        grid=(n_t,),
        in_specs=[row_spec(D_MODEL), pl.BlockSpec((1, 1, tm), lambda i: (i, 0, 0)),
                  _const_spec((1, D_MODEL)),
                  _const_spec(w_main.shape), _const_spec((8, CHUNK)),
                  _const_spec((CHUNK, CHUNK)),
                  _const_spec((8, 1)), _const_spec((D_MODEL, LANES)), _const_spec((1, LANES))],
        out_specs=[head_spec, head_spec, head_t_spec] * 2 + [row_spec(D_MODEL), row_spec(LANES)],
        out_shape=[head_shape, head_shape, head_t_shape] * 2 + [
            jax.ShapeDtypeStruct((BATCH * SEQ, D_MODEL), bf16),
            jax.ShapeDtypeStruct((BATCH * SEQ, LANES), f32)],
        compiler_params=pltpu.CompilerParams(dimension_semantics=("arbitrary",),
                                             vmem_limit_bytes=VMEM_LIMIT),
        name="proj_even",
    )(x2d, pos_col, ng, w_main, gains, _block_ones(64), _inv_freq_col(8),
      w_f, b_f)


def _proj_odd(h_res, pos_res, ng, w_main, gains):
    chunks = (("rope", 128, 0, 0), ("rope", 128, 0, 1), ("rope", 128, 1, 0), ("rope", 128, 1, 1),
              ("plain", 0, 2, 0), ("plain", 0, 2, 1), ("z", 0, 3, 0), ("z", 0, 3, 1))
    head_shape = jax.ShapeDtypeStruct((BATCH, 8, N_RES, SLAB, LANES), bf16)
    head_spec = pl.BlockSpec((1, 8, 8, RES_ROWS, LANES), lambda b, g, c: (b, 0, g, c, 0))
    kern = functools.partial(_proj_kernel, chunks=chunks, rope_half=16, rope_group=128, has_f=False,
                             res_rows=RES_ROWS)
    n_g, n_c = N_RES // 8, SLAB // RES_ROWS
    return pl.pallas_call(
        kern,
        grid=(BATCH, n_g, n_c),
        in_specs=[pl.BlockSpec((1, RES_ROWS, 8, D_MODEL), lambda b, g, c: (b, c, g, 0)),
                  pl.BlockSpec((1, 1, 8 * RES_ROWS), lambda b, g, c: ((b * n_g + g) * n_c + c, 0, 0)),
                  _const_spec((1, D_MODEL)), _const_spec(w_main.shape), _const_spec((8, CHUNK)),
                  _const_spec((CHUNK, CHUNK)),
                  _const_spec((16, 1))],
        out_specs=[head_spec] * 3 + [pl.BlockSpec((1, 8, RES_ROWS, D_MODEL), lambda b, g, c: (b, g, c, 0))],
        out_shape=[head_shape] * 3 + [jax.ShapeDtypeStruct((BATCH, N_RES, SLAB, D_MODEL), bf16)],
        compiler_params=pltpu.CompilerParams(dimension_semantics=("arbitrary",) * 3,
                                             vmem_limit_bytes=VMEM_LIMIT),
        name="proj_odd",
    )(h_res, pos_res, ng, w_main, gains, _block_ones(64), _inv_freq_col(16))


def _split3(x):
    hi = x.astype(bf16).astype(f32)
    r1 = x - hi
    mid = r1.astype(bf16).astype(f32)
    lo = (r1 - mid).astype(bf16).astype(f32)
    return hi, mid, lo


def _split3_lanes(x):
    return jnp.concatenate(_split3(x), axis=1).astype(bf16)


def _cumsum_kernel(logf_ref, place_ref, aug_ref, *, blk):
    row = lax.broadcasted_iota(jnp.int32, (blk, blk), 0)
    col = lax.broadcasted_iota(jnp.int32, (blk, blk), 1)
    lower = (col <= row).astype(bf16)
    carry = jnp.zeros((1, LANES), f32)
    for i in range(SEQ // blk):
        x = logf_ref[i * blk:(i + 1) * blk, :]
        cs3 = jnp.dot(lower, _split3_lanes(x), preferred_element_type=f32)
        cs = (cs3[:, 0:LANES] + cs3[:, LANES:2 * LANES]) + cs3[:, 2 * LANES:3 * LANES] + carry
        carry = cs[blk - 1:blk, :]
        placed = jnp.dot(_split3_lanes(cs * (-LOG2E)), place_ref[...], preferred_element_type=f32)
        for hh in range(4):
            aug_ref[0, hh, i * blk:(i + 1) * blk, :] = placed[:, hh * LANES:(hh + 1) * LANES].astype(bf16)


def _place_matrices():
    pm = np.zeros((3 * LANES, 4 * LANES), np.float32)
    for k in range(3):
        for hh in range(4):
            pm[k * LANES + hh, hh * LANES + k] = 1.0
    return jnp.asarray(pm, dtype=bf16)


def _forget_cumsum(logf):
    return pl.pallas_call(
        functools.partial(_cumsum_kernel, blk=512),
        grid=(BATCH,),
        in_specs=[pl.BlockSpec((SEQ, LANES), lambda b: (b, 0)), _const_spec((3 * LANES, 4 * LANES))],
        out_specs=pl.BlockSpec((1, 4, SEQ, LANES), lambda b: (b, 0, 0, 0)),
        out_shape=jax.ShapeDtypeStruct((BATCH, 4, SEQ, LANES), bf16),
        compiler_params=pltpu.CompilerParams(dimension_semantics=("arbitrary",),
                                             vmem_limit_bytes=VMEM_LIMIT),
        name="forget_cumsum",
    )(logf, _place_matrices())


def _flash_sweep(qq, k_ref, vt_ref, aug_ref, acc_sc, m_sc, i, t):
    n = qq.shape[0]
    m_sc[...] = jnp.full(m_sc.shape, -jnp.inf, f32)
    acc_sc[...] = jnp.zeros(acc_sc.shape, f32)

    def scores(j):
        k = k_ref[0, 0, pl.ds(j * t, t), :]
        if aug_ref is not None:
            k = jnp.concatenate([k, aug_ref[0, 0, pl.ds(j * t, t), :]], axis=1)
        return _nt_dot(k, qq)

    def update(s, j, diag):
        if diag:
            key = lax.broadcasted_iota(jnp.int32, s.shape, 0)
            qry = lax.broadcasted_iota(jnp.int32, s.shape, 1)
            if n > t:
                qry = jnp.where(qry >= t, qry - t, qry)
            s = jnp.where(key <= qry, s, -jnp.inf)
        vt = vt_ref[0, 0, :, pl.ds(j * t, t)]
        vt_ext = jnp.concatenate([vt, jnp.ones((SUM_ROWS, t), bf16)], axis=0)
        m_prev = m_sc[...]
        m_new = jnp.maximum(m_prev, jnp.max(s, axis=0, keepdims=True))
        alpha = jnp.exp2(m_prev - m_new)
        p = jnp.exp2(s - m_new).astype(bf16)
        acc_sc[...] = alpha * acc_sc[...] + jnp.dot(vt_ext, p, preferred_element_type=f32)
        m_sc[...] = m_new

    def group(j0, size, last_diag):
        ss = [scores(j0 + d) for d in range(size)]
        for d in range(size):
            update(ss[d], j0 + d, last_diag and d == size - 1)

    def body(g, carry):
        group(FLASH_GROUP * g, FLASH_GROUP, False)
        return carry

    lax.fori_loop(0, i // FLASH_GROUP, body, 0)
    for rem in range(FLASH_GROUP):
        @pl.when(i % FLASH_GROUP == rem)
        def _(rem=rem):
            group(i - rem, rem + 1, True)


def _diff_attn_kernel(q_ref, k_ref, vt_ref, lam_ref, g_ref, o_ref, acc_sc, m_sc, *,
                      tq, lam_init):
    lam = (jnp.exp(jnp.sum(lam_ref[0:1, :] * lam_ref[1:2, :], axis=1, keepdims=True))
           - jnp.exp(jnp.sum(lam_ref[2:3, :] * lam_ref[3:4, :], axis=1, keepdims=True)) + lam_init)
    lane = lax.broadcasted_iota(jnp.int32, (tq, LANES), 1)

    def q_tile(i, carry):
        q = q_ref[0, 0, pl.ds(i * tq, tq), :]
        zero = jnp.zeros_like(q)
        qq = jnp.concatenate([jnp.where(lane < 64, q, zero), jnp.where(lane >= 64, q, zero)], axis=0)
        _flash_sweep(qq, k_ref, vt_ref, None, acc_sc, m_sc, i, tq)
        ot = acc_sc[0:LANES, :] / acc_sc[LANES:LANES + 1, :]
        ot = ot[:, 0:tq] - lam * ot[:, tq:2 * tq]
        ms = jnp.mean(ot * ot, axis=0, keepdims=True)
        ot = ot * lax.rsqrt(ms + RMS_EPS) * g_ref[...] * (1.0 - lam_init)
        o_ref[0, 0, pl.ds(i * tq, tq), :] = ot.T.astype(bf16)
        return carry

    lax.fori_loop(0, SEQ // tq, q_tile, 0)


def _fox_attn_kernel(q_ref, k_ref, vt_ref, aug_ref, o_ref, acc_sc, m_sc, *, tq):
    lane = lax.broadcasted_iota(jnp.int32, (tq, LANES), 1)
    ones3 = jnp.where(lane < 3, 1.0, 0.0).astype(bf16)

    def q_tile(i, carry):
        qq = jnp.concatenate([q_ref[0, 0, pl.ds(i * tq, tq), :], ones3], axis=1)
        _flash_sweep(qq, k_ref, vt_ref, aug_ref, acc_sc, m_sc, i, tq)
        o_ref[0, 0, pl.ds(i * tq, tq), :] = (
            acc_sc[0:LANES, :] / acc_sc[LANES:LANES + 1, :]).T.astype(bf16)
        return carry

    lax.fori_loop(0, SEQ // tq, q_tile, 0)

def _layer0_attention(q, k, vt, extra, extra_specs, kern, n_stack, tq, name):
    head_spec = pl.BlockSpec((1, 1, SEQ, LANES), lambda b, h: (b, h, 0, 0))
    vt_spec = pl.BlockSpec((1, 1, LANES, SEQ), lambda b, h: (b, h, 0, 0))
    n = n_stack * tq
    return pl.pallas_call(
        kern,
        grid=(BATCH, 4),
        in_specs=[head_spec, head_spec, vt_spec] + extra_specs,
        out_specs=head_spec,
        out_shape=jax.ShapeDtypeStruct((BATCH, 4, SEQ, LANES), bf16),
        scratch_shapes=[pltpu.VMEM((LANES + SUM_ROWS, n), f32), pltpu.VMEM((1, n), f32)],
        compiler_params=pltpu.CompilerParams(
            dimension_semantics=("arbitrary", "arbitrary"), vmem_limit_bytes=VMEM_LIMIT),
        name=name,
    )(q, k, vt, *extra)


def _pattern_bias(dil):
    n_c = N_RES // dil
    bq = max(128 // n_c, 16)
    bk = 2 * bq
    cq, iq = np.divmod(np.arange(n_c * bq), bq)
    ck, jk = np.divmod(np.arange(n_c * bk), bk)
    out = []
    for shift in (0, bq):
        dist = n_c * (shift + iq[:, None] - jk[None, :]) + (cq[:, None] - ck[None, :])
        out.append(np.where((dist >= 0) & (dist <= 128), 0.0, NEG))
    return jnp.asarray(np.stack(out), dtype=f32), n_c, bq


def _dilated_kernel(q_ref, k_ref, v_ref, b16_ref, b4_ref, b1_ref, o_ref, acc_sc, m_sc, l_sc, *, geo):
    def pieces(ref, lead, slabs, start, n):
        parts = [ref[lead + (s, pl.ds(start, n), slice(None))] for s in slabs]
        return parts[0] if len(parts) == 1 else jnp.concatenate(parts, axis=0)

    def load(slabs, ub, bq, bias_ref, first):
        u0 = ub * bq
        ks = jnp.maximum(ub - 1, 0) * bq
        t = dict(slabs=slabs, u0=u0, bq=bq, first=first,
                 q=pieces(q_ref, (0, 0), slabs, u0, bq),
                 k=pieces(k_ref, (0, 0), slabs, ks, 2 * bq),
                 v=pieces(v_ref, (0, 0), slabs, ks, 2 * bq),
                 bias=bias_ref[jnp.minimum(ub, 1)])
        if not first:
            t["m"] = pieces(m_sc, (), slabs, u0, bq)
            t["l"] = pieces(l_sc, (), slabs, u0, bq)
            t["a"] = pieces(acc_sc, (), slabs, u0, bq)
        return t

    def compute(t):
        s = _nt_dot(t["q"], t["k"]) + t["bias"]
        m_cur = jnp.max(s, axis=1, keepdims=True)
        v_ext = jnp.concatenate([t["v"], jnp.ones(t["v"].shape, bf16)], axis=1)
        if t["first"]:
            m_new = jnp.broadcast_to(m_cur, (s.shape[0], LANES))
        else:
            m_new = jnp.maximum(t["m"], m_cur)
            alpha = jnp.exp2(t["m"] - m_new)
        p = jnp.exp2(s - jnp.concatenate([m_new] * (s.shape[1] // LANES), axis=1))
        ext = jnp.dot(p.astype(bf16), v_ext, preferred_element_type=f32)
        a_new, l_new = ext[:, :LANES], ext[:, LANES:]
        if not t["first"]:
            l_new = l_new + alpha * t["l"]
            a_new = a_new + alpha * t["a"]
        return m_new, l_new, a_new

    def store(t, res):
        bq, u0 = t["bq"], t["u0"]
        for n, s_id in enumerate(t["slabs"]):
            rows = slice(n * bq, (n + 1) * bq)
            for ref, val in zip((m_sc, l_sc, acc_sc), res):
                ref[s_id, pl.ds(u0, bq), :] = val[rows]

    def run(tiles):
        results = [compute(t) for t in tiles]
        for t, res in zip(tiles, results):
            store(t, res)

    first = True
    for (dil, n_c, bq), bias_ref in zip(geo, (b16_ref, b4_ref, b1_ref)):
        n_tiles = SLAB // bq
        if n_c == 1:
            def body(rg, carry, bq=bq, bias_ref=bias_ref, first=first, n_tiles=n_tiles):
                run([load([UNROLL_16 * rg + dr], ub, bq, bias_ref, first)
                     for dr in range(UNROLL_16) for ub in range(n_tiles)])
                return carry
            lax.fori_loop(0, N_RES // UNROLL_16, body, 0)
        elif n_c == N_RES:
            def body(pp, carry, bq=bq, bias_ref=bias_ref, first=first):
                run([load(list(range(N_RES)), UNROLL_1 * pp + du, bq, bias_ref, first)
                     for du in range(UNROLL_1)])
                return carry
            lax.fori_loop(0, n_tiles // UNROLL_1, body, 0)
        else:
            def body(up, carry, dil=dil, n_c=n_c, bq=bq, bias_ref=bias_ref, first=first):
                run([load([c * dil + res for c in range(n_c)], UNROLL_4 * up + du, bq, bias_ref, first)
                     for du in range(UNROLL_4) for res in range(dil)])
                return carry
            lax.fori_loop(0, n_tiles // UNROLL_4, body, 0)
        first = False

    def fin(r, carry):
        o_ref[0, 0, r] = (acc_sc[r] / l_sc[r]).astype(bf16)
        return carry
    lax.fori_loop(0, N_RES, fin, 0)


def _dilated_attention(q, k, v):
    biases, geo = [], []
    for dil in (16, 4, 1):
        b, n_c, bq = _pattern_bias(dil)
        biases.append(b)
        geo.append((dil, n_c, bq))
    spec = pl.BlockSpec((1, 1, N_RES, SLAB, LANES), lambda b, h: (b, h, 0, 0, 0))
    return pl.pallas_call(
        functools.partial(_dilated_kernel, geo=tuple(geo)),
        grid=(BATCH, 8),
        in_specs=[spec, spec, spec] + [_const_spec(b.shape) for b in biases],
        out_specs=spec,
        out_shape=jax.ShapeDtypeStruct((BATCH, 8, N_RES, SLAB, LANES), bf16),
        scratch_shapes=[pltpu.VMEM((N_RES, SLAB, LANES), f32)] * 3,
        compiler_params=pltpu.CompilerParams(dimension_semantics=("arbitrary", "arbitrary"),
                                             vmem_limit_bytes=VMEM_LIMIT),
        name="dilated_attn",
    )(q, k, v, *biases)


def _post_kernel(h_ref, *rest, n_o, res_rows):
    o_refs = rest[:n_o]
    z_ref, p_ref, wo_ref, wg_ref, wp_ref, out_ref, mixed_sc, h1_sc = rest[n_o:]
    heads_per = D_MODEL // LANES // n_o
    for n, o_ref in enumerate(o_refs):
        for hh in range(heads_per):
            col = (n * heads_per + hh) * LANES
            if res_rows is None:
                o = o_ref[0, hh].astype(f32)
                z = z_ref[:, col:col + LANES].astype(f32)
                mixed_sc[:, col:col + LANES] = (o * (z * _sigmoid(z))).astype(bf16)
            else:
                for rr in range(8):
                    o = o_ref[0, hh, rr].astype(f32)
                    z = z_ref[0, rr, :, col:col + LANES].astype(f32)
                    mixed_sc[rr * res_rows:(rr + 1) * res_rows, col:col + LANES] = (
                        o * (z * _sigmoid(z))).astype(bf16)
    if res_rows is None:
        h, pb = h_ref[...], p_ref[...]
    else:
        h = jnp.concatenate([h_ref[0, :, rr, :] for rr in range(8)], axis=0)
        pb = jnp.concatenate([p_ref[0, :, rr, :] for rr in range(8)], axis=0)
    pb = pb.astype(bf16)
    h1 = h + jnp.dot(mixed_sc[...], wo_ref[...], preferred_element_type=f32)
    h1_sc[...] = h1
    h1b = h1.astype(bf16)
    for c in range(D_MODEL // CHUNK):
        cols = slice(c * CHUNK, (c + 1) * CHUNK)
        gate = _sigmoid(jnp.dot(h1b, wg_ref[:, cols], preferred_element_type=f32))
        emb = jnp.dot(pb, wp_ref[:, cols], preferred_element_type=f32)
        res = h1_sc[:, cols] + emb * gate
        if res_rows is None:
            out_ref[:, cols] = res
        else:
            for rr in range(8):
                out_ref[0, :, rr, cols] = res[rr * res_rows:(rr + 1) * res_rows]


def _post_even(x2d, oa, ob, z, p0, wo, wg, wp, tm):
    n_t = (BATCH * SEQ) // tm
    per_b = SEQ // tm
    row_spec = lambda w: pl.BlockSpec((tm, w), lambda i: (i, 0))
    head_spec = pl.BlockSpec((1, 4, tm, LANES), lambda i: (i // per_b, 0, i % per_b, 0))
    return pl.pallas_call(
        functools.partial(_post_kernel, n_o=2, res_rows=None),
        grid=(n_t,),
        in_specs=[row_spec(D_MODEL), head_spec, head_spec, row_spec(D_MODEL),
                  pl.BlockSpec((None, tm, PLE_DIM), lambda i: (0, i, 0)),
                  _const_spec(wo.shape), _const_spec(wg.shape), _const_spec(wp.shape)],
        out_specs=row_spec(D_MODEL),
        out_shape=jax.ShapeDtypeStruct((BATCH * SEQ, D_MODEL), f32),
        scratch_shapes=[pltpu.VMEM((tm, D_MODEL), bf16), pltpu.VMEM((tm, D_MODEL), f32)],
        compiler_params=pltpu.CompilerParams(dimension_semantics=("arbitrary",),
                                             vmem_limit_bytes=VMEM_LIMIT),
        name="post_even",
    )(x2d, oa, ob, z, p0, wo, wg, wp)


def _post_odd(h_res, o, z, p_res, wo, wg, wp):
    tm = 8 * RES_ROWS
    res_row = lambda w: pl.BlockSpec((1, RES_ROWS, 8, w), lambda b, g, c: (b, c, g, 0))
    return pl.pallas_call(
        functools.partial(_post_kernel, n_o=1, res_rows=RES_ROWS),
        grid=(BATCH, N_RES // 8, SLAB // RES_ROWS),
        in_specs=[res_row(D_MODEL),
                  pl.BlockSpec((1, 8, 8, RES_ROWS, LANES), lambda b, g, c: (b, 0, g, c, 0)),
                  pl.BlockSpec((1, 8, RES_ROWS, D_MODEL), lambda b, g, c: (b, g, c, 0)),
                  pl.BlockSpec((None, 1, RES_ROWS, 8, PLE_DIM), lambda b, g, c: (1, b, c, g, 0)),
                  _const_spec(wo.shape), _const_spec(wg.shape), _const_spec(wp.shape)],
        out_specs=res_row(D_MODEL),
        out_shape=jax.ShapeDtypeStruct((BATCH, SLAB, N_RES, D_MODEL), f32),
        scratch_shapes=[pltpu.VMEM((tm, D_MODEL), bf16), pltpu.VMEM((tm, D_MODEL), f32)],
        compiler_params=pltpu.CompilerParams(dimension_semantics=("arbitrary",) * 3,
                                             vmem_limit_bytes=VMEM_LIMIT),
        name="post_odd",
    )(h_res, o, z, p_res, wo, wg, wp)


def _tile4(g):
    return jnp.tile(g.astype(f32), CHUNK // g.shape[0])


def kernel(x, p, positions, norm_g, w_in_even, b_forget, qn_a, kn_a, qn_b, kn_b, lam_q1, lam_k1,
           lam_q2, lam_k2, subln_g, w_out_even, w_in_odd, qn_c, kn_c, w_out_odd, w_ple, w_ple_gate):
    tm = 512
    tq_diff, tq_fox = 512, 512
    n_tok = BATCH * SEQ
    x2d = x.reshape(n_tok, D_MODEL)
    pos_f = positions.astype(f32)

    w0 = w_in_even[0]
    f_lo, f_hi = 3072, 3076
    w_main0 = jnp.concatenate([w0[:, :f_lo], w0[:, f_hi:]], axis=1).astype(bf16)
    w_f = jnp.pad(w0[:, f_lo:f_hi], ((0, 0), (0, LANES - 4))).astype(bf16)
    b_f = jnp.pad(b_forget[0].astype(f32), (0, LANES - 4)).reshape(1, LANES)
    ones_row = jnp.ones((CHUNK,), f32)
    gains0 = jnp.stack([_tile4(qn_a[0]) * (64 ** -0.5 * LOG2E), _tile4(kn_a[0]), ones_row,
                        _tile4(qn_b[0]) * (HEAD_DIM ** -0.5 * LOG2E), _tile4(kn_b[0]), ones_row,
                        ones_row, ones_row])
    qa, ka, va, qb, kb, vb, z0, logf = _proj_even(
        x2d, pos_f.reshape(n_tok // tm, 1, tm), norm_g[0].reshape(1, D_MODEL).astype(f32), w_main0, gains0,
        w_f, b_f, tm)
    cum = _forget_cumsum(logf)

    lam_rows = jnp.pad(jnp.stack([lam_q1[0], lam_k1[0], lam_q2[0], lam_k2[0]]).astype(f32),
                       ((0, 4), (0, LANES - 64)))
    lam_init = 0.8 - 0.6 * math.exp(-0.3 * 0)
    oa = _layer0_attention(
        qa, ka, va, [lam_rows, subln_g[0].reshape(LANES, 1).astype(f32)],
        [_const_spec((8, LANES)), _const_spec((LANES, 1))],
        functools.partial(_diff_attn_kernel, tq=tq_diff, lam_init=lam_init),
        2, tq_diff, "diff_attn")
    ob = _layer0_attention(
        qb, kb, vb, [cum], [pl.BlockSpec((1, 1, SEQ, LANES), lambda b, h: (b, h, 0, 0))],
        functools.partial(_fox_attn_kernel, tq=tq_fox), 1, tq_fox, "fox_attn")
    h1 = _post_even(x2d, oa, ob, z0, p.reshape(2, n_tok, PLE_DIM), w_out_even[0].astype(bf16),
                    w_ple_gate[0].astype(bf16), w_ple[0].astype(bf16), tm)

    h_res = h1.reshape(BATCH, SLAB, N_RES, D_MODEL)
    pos_res = (pos_f.reshape(BATCH, SLAB // RES_ROWS, RES_ROWS, N_RES // 8, 8)
               .transpose(0, 3, 1, 4, 2).reshape(-1, 1, 8 * RES_ROWS))
    ones_row2 = jnp.ones((CHUNK,), f32)
    gq = _tile4(qn_c[0]) * (HEAD_DIM ** -0.5 * LOG2E)
    gk = _tile4(kn_c[0])
    gains1 = jnp.stack([gq, gq, gk, gk, ones_row2, ones_row2, ones_row2, ones_row2])
    q1, k1, v1, z1 = _proj_odd(h_res, pos_res, norm_g[1].reshape(1, D_MODEL).astype(f32),
                               w_in_odd[0].astype(bf16), gains1)
    o1 = _dilated_attention(q1, k1, v1)
    out = _post_odd(h_res, o1, z1, p.reshape(2, BATCH, SLAB, N_RES, PLE_DIM),
                    w_out_odd[0].astype(bf16), w_ple_gate[1].astype(bf16), w_ple[1].astype(bf16))
    return out.reshape(BATCH, SEQ, D_MODEL)
```

```python
import functools
import math

import numpy as np
import jax
import jax.numpy as jnp
from jax import lax
from jax.experimental import pallas as pl
from jax.experimental.pallas import tpu as pltpu

D_MODEL = 1024
BATCH = 8
SEQ = 4096
PLE_DIM = 256
HEAD_DIM = 128
ROPE_THETA = 500000.0
RMS_EPS = 1e-6
LANES = 128
CHUNK = 512
N_RES = 16
SLAB = SEQ // N_RES
RES_ROWS = 64
VMEM_LIMIT = 56 * 1024 * 1024
NEG = -1e30
LOG2E = math.log2(math.e)
FLASH_GROUP = 4
SUM_ROWS = 16
UNROLL_16, UNROLL_4, UNROLL_1 = 16, 8, 16

bf16 = jnp.bfloat16
f32 = jnp.float32


def _nt_dot(a, b):
    return lax.dot_general(a, b, (((1,), (1,)), ((), ())), preferred_element_type=f32)


def _sigmoid(x):
    return 1.0 / (1.0 + jnp.exp(-x))


def _proj_kernel(x_ref, pos_ref, ng_ref, w_ref, gains_ref, ones_ref, invf_ref,
                 *rest, chunks, rope_half, rope_group, has_f, res_rows):
    if has_f:
        wf_ref, bf_ref = rest[0], rest[1]
        out_refs = rest[2:]
    else:
        out_refs = rest
    if res_rows is None:
        x = x_ref[...]
    else:
        x = jnp.concatenate([x_ref[0, :, rr, :] for rr in range(8)], axis=0)
    ms = jnp.mean(x * x, axis=-1, keepdims=True)
    hn = (x * lax.rsqrt(ms + RMS_EPS) * ng_ref[...]).astype(bf16)

    tm = x.shape[0]
    ang = invf_ref[...] * pos_ref[...].reshape(1, tm)
    cos_c, sin_c = jnp.cos(ang), jnp.sin(ang)
    zeros_c = jnp.zeros_like(ang)
    rest_rows = rope_group - 2 * rope_half
    reps = LANES // rope_group

    def lane_table(first, second, fill):
        blk = [first, second, jnp.full((rest_rows, tm), fill, f32)]
        return jnp.concatenate(blk * reps, axis=0).T

    cosv = lane_table(cos_c, cos_c, 1.0)
    sin_hi = lane_table(zeros_c, sin_c, 0.0)
    sin_lo = lane_table(-sin_c, zeros_c, 0.0)

    def put_head(o_ref, head, yb):
        if res_rows is None:
            o_ref[0, head] = yb
        else:
            for rr in range(8):
                o_ref[0, head, rr] = yb[rr * res_rows:(rr + 1) * res_rows]

    def put_cols(o_ref, col, yb):
        if res_rows is None:
            o_ref[:, col:col + LANES] = yb
        else:
            for rr in range(8):
                o_ref[0, rr, :, col:col + LANES] = yb[rr * res_rows:(rr + 1) * res_rows]

    for c, (kind, group, oi, slot) in enumerate(chunks):
        acc = jnp.dot(hn, w_ref[:, c * CHUNK:(c + 1) * CHUNK], preferred_element_type=f32)
        if kind in ("rope", "norm") and group < LANES:
            ss = jnp.dot((acc * acc).astype(bf16), ones_ref[...], preferred_element_type=f32)
            acc = acc * lax.rsqrt(ss * (1.0 / group) + RMS_EPS) * gains_ref[c:c + 1, :]
        o_ref = out_refs[oi]
        for hh in range(CHUNK // LANES):
            yb = acc[:, hh * LANES:(hh + 1) * LANES]
            if kind in ("rope", "norm") and group == LANES:
                ms = jnp.mean(yb * yb, axis=-1, keepdims=True)
                yb = yb * lax.rsqrt(ms + RMS_EPS) * gains_ref[c:c + 1, hh * LANES:(hh + 1) * LANES]
            if kind == "rope":
                yb = (yb * cosv + pltpu.roll(yb, rope_half, 1) * sin_hi
                      + pltpu.roll(yb, LANES - rope_half, 1) * sin_lo)
            if kind == "plain_t":
                o_ref[0, slot * (CHUNK // LANES) + hh] = yb.T.astype(bf16)
            elif kind == "z":
                put_cols(o_ref, slot * CHUNK + hh * LANES, yb.astype(bf16))
            else:
                put_head(o_ref, slot * (CHUNK // LANES) + hh, yb.astype(bf16))

    if has_f:
        f = jnp.dot(hn, wf_ref[...], preferred_element_type=f32) + bf_ref[...]
        out_refs[-1][...] = jnp.minimum(f, 0.0) - jnp.log1p(jnp.exp(-jnp.abs(f)))


def _inv_freq_col(half):
    return jnp.exp(-math.log(ROPE_THETA) * jnp.arange(half, dtype=f32) / half).reshape(half, 1)


def _block_ones(group):
    i = np.arange(CHUNK)
    return jnp.asarray((i[:, None] // group == i[None, :] // group).astype(np.float32), dtype=bf16)


def _const_spec(shape):
    nd = len(shape)
    return pl.BlockSpec(shape, lambda *_: (0,) * nd)


def _proj_even(x2d, pos_col, ng, w_main, gains, w_f, b_f, tm):
    n_t = (BATCH * SEQ) // tm
    per_b = SEQ // tm
    chunks = (("rope", 64, 0, 0), ("rope", 64, 1, 0), ("plain_t", 0, 2, 0),
              ("norm", 128, 3, 0), ("norm", 128, 4, 0), ("plain_t", 0, 5, 0),
              ("z", 0, 6, 0), ("z", 0, 6, 1))
    head_shape = jax.ShapeDtypeStruct((BATCH, 4, SEQ, LANES), bf16)
    head_spec = pl.BlockSpec((1, 4, tm, LANES), lambda i: (i // per_b, 0, i % per_b, 0))
    head_t_shape = jax.ShapeDtypeStruct((BATCH, 4, LANES, SEQ), bf16)
    head_t_spec = pl.BlockSpec((1, 4, LANES, tm), lambda i: (i // per_b, 0, 0, i % per_b))
    row_spec = lambda w: pl.BlockSpec((tm, w), lambda i: (i, 0))
    kern = functools.partial(_proj_kernel, chunks=chunks, rope_half=8, rope_group=64, has_f=True,
                             res_rows=None)
    return pl.pallas_call(
        kern,
        grid=(n_t,),
        in_specs=[row_spec(D_MODEL), pl.BlockSpec((1, 1, tm), lambda i: (i, 0, 0)),
                  _const_spec((1, D_MODEL)),
                  _const_spec(w_main.shape), _const_spec((8, CHUNK)),
                  _const_spec((CHUNK, CHUNK)),
                  _const_spec((8, 1)), _const_spec((D_MODEL, LANES)), _const_spec((1, LANES))],
        out_specs=[head_spec, head_spec, head_t_spec] * 2 + [row_spec(D_MODEL), row_spec(LANES)],
        out_shape=[head_shape, head_shape, head_t_shape] * 2 + [
            jax.ShapeDtypeStruct((BATCH * SEQ, D_MODEL), bf16),
            jax.ShapeDtypeStruct((BATCH * SEQ, LANES), f32)],
        compiler_params=pltpu.CompilerParams(dimension_semantics=("arbitrary",),
                                             vmem_limit_bytes=VMEM_LIMIT),
        name="proj_even",
    )(x2d, pos_col, ng, w_main, gains, _block_ones(64), _inv_freq_col(8),
      w_f, b_f)


def _proj_odd(h_res, pos_res, ng, w_main, gains):
    chunks = (("rope", 128, 0, 0), ("rope", 128, 0, 1), ("rope", 128, 1, 0), ("rope", 128, 1, 1),
              ("plain", 0, 2, 0), ("plain", 0, 2, 1), ("z", 0, 3, 0), ("z", 0, 3, 1))
    head_shape = jax.ShapeDtypeStruct((BATCH, 8, N_RES, SLAB, LANES), bf16)
    head_spec = pl.BlockSpec((1, 8, 8, RES_ROWS, LANES), lambda b, g, c: (b, 0, g, c, 0))
    kern = functools.partial(_proj_kernel, chunks=chunks, rope_half=16, rope_group=128, has_f=False,
                             res_rows=RES_ROWS)
    n_g, n_c = N_RES // 8, SLAB // RES_ROWS
    return pl.pallas_call(
        kern,
        grid=(BATCH, n_g, n_c),
        in_specs=[pl.BlockSpec((1, RES_ROWS, 8, D_MODEL), lambda b, g, c: (b, c, g, 0)),
                  pl.BlockSpec((1, 1, 8 * RES_ROWS), lambda b, g, c: ((b * n_g + g) * n_c + c, 0, 0)),
                  _const_spec((1, D_MODEL)), _const_spec(w_main.shape), _const_spec((8, CHUNK)),
                  _const_spec((CHUNK, CHUNK)),
                  _const_spec((16, 1))],
        out_specs=[head_spec] * 3 + [pl.BlockSpec((1, 8, RES_ROWS, D_MODEL), lambda b, g, c: (b, g, c, 0))],
        out_shape=[head_shape] * 3 + [jax.ShapeDtypeStruct((BATCH, N_RES, SLAB, D_MODEL), bf16)],
        compiler_params=pltpu.CompilerParams(dimension_semantics=("arbitrary",) * 3,
                                             vmem_limit_bytes=VMEM_LIMIT),
        name="proj_odd",
    )(h_res, pos_res, ng, w_main, gains, _block_ones(64), _inv_freq_col(16))


def _split3(x):
    hi = x.astype(bf16).astype(f32)
    r1 = x - hi
    mid = r1.astype(bf16).astype(f32)
    lo = (r1 - mid).astype(bf16).astype(f32)
    return hi, mid, lo


def _split3_lanes(x):
    return jnp.concatenate(_split3(x), axis=1).astype(bf16)


def _cumsum_kernel(logf_ref, place_ref, aug_ref, *, blk):
    row = lax.broadcasted_iota(jnp.int32, (blk, blk), 0)
    col = lax.broadcasted_iota(jnp.int32, (blk, blk), 1)
    lower = (col <= row).astype(bf16)
    carry = jnp.zeros((1, LANES), f32)
    for i in range(SEQ // blk):
        x = logf_ref[i * blk:(i + 1) * blk, :]
        cs3 = jnp.dot(lower, _split3_lanes(x), preferred_element_type=f32)
        cs = (cs3[:, 0:LANES] + cs3[:, LANES:2 * LANES]) + cs3[:, 2 * LANES:3 * LANES] + carry
        carry = cs[blk - 1:blk, :]
        placed = jnp.dot(_split3_lanes(cs * (-LOG2E)), place_ref[...], preferred_element_type=f32)
        for hh in range(4):
            aug_ref[0, hh, i * blk:(i + 1) * blk, :] = placed[:, hh * LANES:(hh + 1) * LANES].astype(bf16)


def _place_matrices():
    pm = np.zeros((3 * LANES, 4 * LANES), np.float32)
    for k in range(3):
        for hh in range(4):
            pm[k * LANES + hh, hh * LANES + k] = 1.0
    return jnp.asarray(pm, dtype=bf16)


def _forget_cumsum(logf):
    return pl.pallas_call(
        functools.partial(_cumsum_kernel, blk=512),
        grid=(BATCH,),
        in_specs=[pl.BlockSpec((SEQ, LANES), lambda b: (b, 0)), _const_spec((3 * LANES, 4 * LANES))],
        out_specs=pl.BlockSpec((1, 4, SEQ, LANES), lambda b: (b, 0, 0, 0)),
        out_shape=jax.ShapeDtypeStruct((BATCH, 4, SEQ, LANES), bf16),
        compiler_params=pltpu.CompilerParams(dimension_semantics=("arbitrary",),
                                             vmem_limit_bytes=VMEM_LIMIT),
        name="forget_cumsum",
    )(logf, _place_matrices())


def _flash_sweep(qq, k_ref, vt_ref, aug_ref, acc_sc, m_sc, i, t):
    n = qq.shape[0]
    m_sc[...] = jnp.full(m_sc.shape, -jnp.inf, f32)
    acc_sc[...] = jnp.zeros(acc_sc.shape, f32)

    def scores(j):
        k = k_ref[0, 0, pl.ds(j * t, t), :]
        if aug_ref is not None:
            k = jnp.concatenate([k, aug_ref[0, 0, pl.ds(j * t, t), :]], axis=1)
        return _nt_dot(k, qq)

    def update(s, j, diag):
        if diag:
            key = lax.broadcasted_iota(jnp.int32, s.shape, 0)
            qry = lax.broadcasted_iota(jnp.int32, s.shape, 1)
            if n > t:
                qry = jnp.where(qry >= t, qry - t, qry)
            s = jnp.where(key <= qry, s, -jnp.inf)
        vt = vt_ref[0, 0, :, pl.ds(j * t, t)]
        vt_ext = jnp.concatenate([vt, jnp.ones((SUM_ROWS, t), bf16)], axis=0)
        m_prev = m_sc[...]
        m_new = jnp.maximum(m_prev, jnp.max(s, axis=0, keepdims=True))
        alpha = jnp.exp2(m_prev - m_new)
        p = jnp.exp2(s - m_new).astype(bf16)
        acc_sc[...] = alpha * acc_sc[...] + jnp.dot(vt_ext, p, preferred_element_type=f32)
        m_sc[...] = m_new

    def group(j0, size, last_diag):
        ss = [scores(j0 + d) for d in range(size)]
        for d in range(size):
            update(ss[d], j0 + d, last_diag and d == size - 1)

    for g in range(i // FLASH_GROUP):
        group(FLASH_GROUP * g, FLASH_GROUP, False)
    rem = i % FLASH_GROUP
    group(i - rem, rem + 1, True)


def _diff_attn_kernel(q_ref, k_ref, vt_ref, lam_ref, g_ref, o_ref, acc_sc, m_sc, *,
                      tq, lam_init):
    lam = (jnp.exp(jnp.sum(lam_ref[0:1, :] * lam_ref[1:2, :], axis=1, keepdims=True))
           - jnp.exp(jnp.sum(lam_ref[2:3, :] * lam_ref[3:4, :], axis=1, keepdims=True)) + lam_init)
    lane = lax.broadcasted_iota(jnp.int32, (tq, LANES), 1)

    for i in range(SEQ // tq):
        q = q_ref[0, 0, i * tq:(i + 1) * tq, :]
        zero = jnp.zeros_like(q)
        qq = jnp.concatenate([jnp.where(lane < 64, q, zero), jnp.where(lane >= 64, q, zero)], axis=0)
        _flash_sweep(qq, k_ref, vt_ref, None, acc_sc.at[i], m_sc.at[i], i, tq)
        ot = acc_sc[i, 0:LANES, :] / acc_sc[i, LANES:LANES + 1, :]
        ot = ot[:, 0:tq] - lam * ot[:, tq:2 * tq]
        ms = jnp.mean(ot * ot, axis=0, keepdims=True)
        ot = ot * lax.rsqrt(ms + RMS_EPS) * g_ref[...] * (1.0 - lam_init)
        o_ref[0, 0, i * tq:(i + 1) * tq, :] = ot.T.astype(bf16)


def _fox_attn_kernel(q_ref, k_ref, vt_ref, aug_ref, o_ref, acc_sc, m_sc, *, tq):
    lane = lax.broadcasted_iota(jnp.int32, (tq, LANES), 1)
    ones3 = jnp.where(lane < 3, 1.0, 0.0).astype(bf16)

    for i in range(SEQ // tq):
        qq = jnp.concatenate([q_ref[0, 0, i * tq:(i + 1) * tq, :], ones3], axis=1)
        _flash_sweep(qq, k_ref, vt_ref, aug_ref, acc_sc.at[i], m_sc.at[i], i, tq)
        o_ref[0, 0, i * tq:(i + 1) * tq, :] = (
            acc_sc[i, 0:LANES, :] / acc_sc[i, LANES:LANES + 1, :]).T.astype(bf16)

def _layer0_attention(q, k, vt, extra, extra_specs, kern, n_stack, tq, name):
    head_spec = pl.BlockSpec((1, 1, SEQ, LANES), lambda b, h: (b, h, 0, 0))
    vt_spec = pl.BlockSpec((1, 1, LANES, SEQ), lambda b, h: (b, h, 0, 0))
    n = n_stack * tq
    return pl.pallas_call(
        kern,
        grid=(BATCH, 4),
        in_specs=[head_spec, head_spec, vt_spec] + extra_specs,
        out_specs=head_spec,
        out_shape=jax.ShapeDtypeStruct((BATCH, 4, SEQ, LANES), bf16),
        scratch_shapes=[pltpu.VMEM((SEQ // tq, LANES + SUM_ROWS, n), f32),
                        pltpu.VMEM((SEQ // tq, 1, n), f32)],
        compiler_params=pltpu.CompilerParams(
            dimension_semantics=("arbitrary", "arbitrary"), vmem_limit_bytes=VMEM_LIMIT),
        name=name,
    )(q, k, vt, *extra)


def _pattern_bias(dil):
    n_c = N_RES // dil
    bq = max(128 // n_c, 16)
    bk = 2 * bq
    cq, iq = np.divmod(np.arange(n_c * bq), bq)
    ck, jk = np.divmod(np.arange(n_c * bk), bk)
    out = []
    for shift in (0, bq):
        dist = n_c * (shift + iq[:, None] - jk[None, :]) + (cq[:, None] - ck[None, :])
        out.append(np.where((dist >= 0) & (dist <= 128), 0.0, NEG))
    return jnp.asarray(np.stack(out), dtype=f32), n_c, bq


def _dilated_kernel(q_ref, k_ref, v_ref, b16_ref, b4_ref, b1_ref, o_ref, acc_sc, m_sc, l_sc, *, geo):
    def pieces(ref, lead, slabs, start, n):
        parts = [ref[lead + (s, pl.ds(start, n), slice(None))] for s in slabs]
        return parts[0] if len(parts) == 1 else jnp.concatenate(parts, axis=0)

    def load(slabs, ub, bq, bias_ref, first):
        u0 = ub * bq
        ks = jnp.maximum(ub - 1, 0) * bq
        t = dict(slabs=slabs, u0=u0, bq=bq, first=first,
                 q=pieces(q_ref, (0, 0), slabs, u0, bq),
                 k=pieces(k_ref, (0, 0), slabs, ks, 2 * bq),
                 v=pieces(v_ref, (0, 0), slabs, ks, 2 * bq),
                 bias=bias_ref[jnp.minimum(ub, 1)])
        if not first:
            t["m"] = pieces(m_sc, (), slabs, u0, bq)
            t["l"] = pieces(l_sc, (), slabs, u0, bq)
            t["a"] = pieces(acc_sc, (), slabs, u0, bq)
        return t

    def compute(t):
        s = _nt_dot(t["q"], t["k"]) + t["bias"]
        m_cur = jnp.max(s, axis=1, keepdims=True)
        v_ext = jnp.concatenate([t["v"], jnp.ones(t["v"].shape, bf16)], axis=1)
        if t["first"]:
            m_new = jnp.broadcast_to(m_cur, (s.shape[0], LANES))
        else:
            m_new = jnp.maximum(t["m"], m_cur)
            alpha = jnp.exp2(t["m"] - m_new)
        p = jnp.exp2(s - jnp.concatenate([m_new] * (s.shape[1] // LANES), axis=1))
        ext = jnp.dot(p.astype(bf16), v_ext, preferred_element_type=f32)
        a_new, l_new = ext[:, :LANES], ext[:, LANES:]
        if not t["first"]:
            l_new = l_new + alpha * t["l"]
            a_new = a_new + alpha * t["a"]
        return m_new, l_new, a_new

    def store(t, res):
        bq, u0 = t["bq"], t["u0"]
        for n, s_id in enumerate(t["slabs"]):
            rows = slice(n * bq, (n + 1) * bq)
            for ref, val in zip((m_sc, l_sc, acc_sc), res):
                ref[s_id, pl.ds(u0, bq), :] = val[rows]

    def run(tiles):
        results = [compute(t) for t in tiles]
        for t, res in zip(tiles, results):
            store(t, res)

    first = True
    for (dil, n_c, bq), bias_ref in zip(geo, (b16_ref, b4_ref, b1_ref)):
        n_tiles = SLAB // bq
        if n_c == 1:
            def body(rg, carry, bq=bq, bias_ref=bias_ref, first=first, n_tiles=n_tiles):
                run([load([UNROLL_16 * rg + dr], ub, bq, bias_ref, first)
                     for dr in range(UNROLL_16) for ub in range(n_tiles)])
                return carry
            lax.fori_loop(0, N_RES // UNROLL_16, body, 0)
        elif n_c == N_RES:
            def body(pp, carry, bq=bq, bias_ref=bias_ref, first=first):
                run([load(list(range(N_RES)), UNROLL_1 * pp + du, bq, bias_ref, first)
                     for du in range(UNROLL_1)])
                return carry
            lax.fori_loop(0, n_tiles // UNROLL_1, body, 0)
        else:
            def body(up, carry, dil=dil, n_c=n_c, bq=bq, bias_ref=bias_ref, first=first):
                run([load([c * dil + res for c in range(n_c)], UNROLL_4 * up + du, bq, bias_ref, first)
                     for du in range(UNROLL_4) for res in range(dil)])
                return carry
            lax.fori_loop(0, n_tiles // UNROLL_4, body, 0)
        first = False

    def fin(r, carry):
        o_ref[0, 0, r] = (acc_sc[r] / l_sc[r]).astype(bf16)
        return carry
    lax.fori_loop(0, N_RES, fin, 0)


def _dilated_attention(q, k, v):
    biases, geo = [], []
    for dil in (16, 4, 1):
        b, n_c, bq = _pattern_bias(dil)
        biases.append(b)
        geo.append((dil, n_c, bq))
    spec = pl.BlockSpec((1, 1, N_RES, SLAB, LANES), lambda b, h: (b, h, 0, 0, 0))
    return pl.pallas_call(
        functools.partial(_dilated_kernel, geo=tuple(geo)),
        grid=(BATCH, 8),
        in_specs=[spec, spec, spec] + [_const_spec(b.shape) for b in biases],
        out_specs=spec,
        out_shape=jax.ShapeDtypeStruct((BATCH, 8, N_RES, SLAB, LANES), bf16),
        scratch_shapes=[pltpu.VMEM((N_RES, SLAB, LANES), f32)] * 3,
        compiler_params=pltpu.CompilerParams(dimension_semantics=("arbitrary", "arbitrary"),
                                             vmem_limit_bytes=VMEM_LIMIT),
        name="dilated_attn",
    )(q, k, v, *biases)


def _post_kernel(h_ref, *rest, n_o, res_rows):
    o_refs = rest[:n_o]
    z_ref, p_ref, wo_ref, wg_ref, wp_ref, out_ref, mixed_sc, h1_sc = rest[n_o:]
    heads_per = D_MODEL // LANES // n_o
    for n, o_ref in enumerate(o_refs):
        for hh in range(heads_per):
            col = (n * heads_per + hh) * LANES
            if res_rows is None:
                o = o_ref[0, hh].astype(f32)
                z = z_ref[:, col:col + LANES].astype(f32)
                mixed_sc[:, col:col + LANES] = (o * (z * _sigmoid(z))).astype(bf16)
            else:
                for rr in range(8):
                    o = o_ref[0, hh, rr].astype(f32)
                    z = z_ref[0, rr, :, col:col + LANES].astype(f32)
                    mixed_sc[rr * res_rows:(rr + 1) * res_rows, col:col + LANES] = (
                        o * (z * _sigmoid(z))).astype(bf16)
    if res_rows is None:
        h, pb = h_ref[...], p_ref[...]
    else:
        h = jnp.concatenate([h_ref[0, :, rr, :] for rr in range(8)], axis=0)
        pb = jnp.concatenate([p_ref[0, :, rr, :] for rr in range(8)], axis=0)
    pb = pb.astype(bf16)
    h1 = h + jnp.dot(mixed_sc[...], wo_ref[...], preferred_element_type=f32)
    h1_sc[...] = h1
    h1b = h1.astype(bf16)
    for c in range(D_MODEL // CHUNK):
        cols = slice(c * CHUNK, (c + 1) * CHUNK)
        gate = _sigmoid(jnp.dot(h1b, wg_ref[:, cols], preferred_element_type=f32))
        emb = jnp.dot(pb, wp_ref[:, cols], preferred_element_type=f32)
        res = h1_sc[:, cols] + emb * gate
        if res_rows is None:
            out_ref[:, cols] = res
        else:
            for rr in range(8):
                out_ref[0, :, rr, cols] = res[rr * res_rows:(rr + 1) * res_rows]


def _post_even(x2d, oa, ob, z, p0, wo, wg, wp, tm):
    n_t = (BATCH * SEQ) // tm
    per_b = SEQ // tm
    row_spec = lambda w: pl.BlockSpec((tm, w), lambda i: (i, 0))
    head_spec = pl.BlockSpec((1, 4, tm, LANES), lambda i: (i // per_b, 0, i % per_b, 0))
    return pl.pallas_call(
        functools.partial(_post_kernel, n_o=2, res_rows=None),
        grid=(n_t,),
        in_specs=[row_spec(D_MODEL), head_spec, head_spec, row_spec(D_MODEL),
                  pl.BlockSpec((None, tm, PLE_DIM), lambda i: (0, i, 0)),
                  _const_spec(wo.shape), _const_spec(wg.shape), _const_spec(wp.shape)],
        out_specs=row_spec(D_MODEL),
        out_shape=jax.ShapeDtypeStruct((BATCH * SEQ, D_MODEL), f32),
        scratch_shapes=[pltpu.VMEM((tm, D_MODEL), bf16), pltpu.VMEM((tm, D_MODEL), f32)],
        compiler_params=pltpu.CompilerParams(dimension_semantics=("arbitrary",),
                                             vmem_limit_bytes=VMEM_LIMIT),
        name="post_even",
    )(x2d, oa, ob, z, p0, wo, wg, wp)


def _post_odd(h_res, o, z, p_res, wo, wg, wp):
    tm = 8 * RES_ROWS
    res_row = lambda w: pl.BlockSpec((1, RES_ROWS, 8, w), lambda b, g, c: (b, c, g, 0))
    return pl.pallas_call(
        functools.partial(_post_kernel, n_o=1, res_rows=RES_ROWS),
        grid=(BATCH, N_RES // 8, SLAB // RES_ROWS),
        in_specs=[res_row(D_MODEL),
                  pl.BlockSpec((1, 8, 8, RES_ROWS, LANES), lambda b, g, c: (b, 0, g, c, 0)),
                  pl.BlockSpec((1, 8, RES_ROWS, D_MODEL), lambda b, g, c: (b, g, c, 0)),
                  pl.BlockSpec((None, 1, RES_ROWS, 8, PLE_DIM), lambda b, g, c: (1, b, c, g, 0)),
                  _const_spec(wo.shape), _const_spec(wg.shape), _const_spec(wp.shape)],
        out_specs=res_row(D_MODEL),
        out_shape=jax.ShapeDtypeStruct((BATCH, SLAB, N_RES, D_MODEL), f32),
        scratch_shapes=[pltpu.VMEM((tm, D_MODEL), bf16), pltpu.VMEM((tm, D_MODEL), f32)],
        compiler_params=pltpu.CompilerParams(dimension_semantics=("arbitrary",) * 3,
                                             vmem_limit_bytes=VMEM_LIMIT),
        name="post_odd",
    )(h_res, o, z, p_res, wo, wg, wp)


def _tile4(g):
    return jnp.tile(g.astype(f32), CHUNK // g.shape[0])


def kernel(x, p, positions, norm_g, w_in_even, b_forget, qn_a, kn_a, qn_b, kn_b, lam_q1, lam_k1,
           lam_q2, lam_k2, subln_g, w_out_even, w_in_odd, qn_c, kn_c, w_out_odd, w_ple, w_ple_gate):
    tm = 512
    tq_diff, tq_fox = 512, 512
    n_tok = BATCH * SEQ
    x2d = x.reshape(n_tok, D_MODEL)
    pos_f = positions.astype(f32)

    w0 = w_in_even[0]
    f_lo, f_hi = 3072, 3076
    w_main0 = jnp.concatenate([w0[:, :f_lo], w0[:, f_hi:]], axis=1).astype(bf16)
    w_f = jnp.pad(w0[:, f_lo:f_hi], ((0, 0), (0, LANES - 4))).astype(bf16)
    b_f = jnp.pad(b_forget[0].astype(f32), (0, LANES - 4)).reshape(1, LANES)
    ones_row = jnp.ones((CHUNK,), f32)
    gains0 = jnp.stack([_tile4(qn_a[0]) * (64 ** -0.5 * LOG2E), _tile4(kn_a[0]), ones_row,
                        _tile4(qn_b[0]) * (HEAD_DIM ** -0.5 * LOG2E), _tile4(kn_b[0]), ones_row,
                        ones_row, ones_row])
    qa, ka, va, qb, kb, vb, z0, logf = _proj_even(
        x2d, pos_f.reshape(n_tok // tm, 1, tm), norm_g[0].reshape(1, D_MODEL).astype(f32), w_main0, gains0,
        w_f, b_f, tm)
    cum = _forget_cumsum(logf)

    lam_rows = jnp.pad(jnp.stack([lam_q1[0], lam_k1[0], lam_q2[0], lam_k2[0]]).astype(f32),
                       ((0, 4), (0, LANES - 64)))
    lam_init = 0.8 - 0.6 * math.exp(-0.3 * 0)
    oa = _layer0_attention(
        qa, ka, va, [lam_rows, subln_g[0].reshape(LANES, 1).astype(f32)],
        [_const_spec((8, LANES)), _const_spec((LANES, 1))],
        functools.partial(_diff_attn_kernel, tq=tq_diff, lam_init=lam_init),
        2, tq_diff, "diff_attn")
    ob = _layer0_attention(
        qb, kb, vb, [cum], [pl.BlockSpec((1, 1, SEQ, LANES), lambda b, h: (b, h, 0, 0))],
        functools.partial(_fox_attn_kernel, tq=tq_fox), 1, tq_fox, "fox_attn")
    h1 = _post_even(x2d, oa, ob, z0, p.reshape(2, n_tok, PLE_DIM), w_out_even[0].astype(bf16),
                    w_ple_gate[0].astype(bf16), w_ple[0].astype(bf16), tm)

    h_res = h1.reshape(BATCH, SLAB, N_RES, D_MODEL)
    pos_res = (pos_f.reshape(BATCH, SLAB // RES_ROWS, RES_ROWS, N_RES // 8, 8)
               .transpose(0, 3, 1, 4, 2).reshape(-1, 1, 8 * RES_ROWS))
    ones_row2 = jnp.ones((CHUNK,), f32)
    gq = _tile4(qn_c[0]) * (HEAD_DIM ** -0.5 * LOG2E)
    gk = _tile4(kn_c[0])
    gains1 = jnp.stack([gq, gq, gk, gk, ones_row2, ones_row2, ones_row2, ones_row2])
    q1, k1, v1, z1 = _proj_odd(h_res, pos_res, norm_g[1].reshape(1, D_MODEL).astype(f32),
                               w_in_odd[0].astype(bf16), gains1)
    o1 = _dilated_attention(q1, k1, v1)
    out = _post_odd(h_res, o1, z1, p.reshape(2, BATCH, SLAB, N_RES, PLE_DIM),
                    w_out_odd[0].astype(bf16), w_ple_gate[1].astype(bf16), w_ple[1].astype(bf16))
    return out.reshape(BATCH, SEQ, D_MODEL)
```

```python
import functools
import math

import numpy as np
import jax
import jax.numpy as jnp
from jax import lax
from jax.experimental import pallas as pl
from jax.experimental.pallas import tpu as pltpu

D_MODEL = 1024
BATCH = 8
SEQ = 4096
PLE_DIM = 256
HEAD_DIM = 128
ROPE_THETA = 500000.0
RMS_EPS = 1e-6
LANES = 128
CHUNK = 512
N_RES = 16
SLAB = SEQ // N_RES
RES_ROWS = 64
VMEM_LIMIT = 56 * 1024 * 1024
NEG = -1e30
LOG2E = math.log2(math.e)
FLASH_GROUP = 8
SUM_ROWS = 16
UNROLL_16, UNROLL_4, UNROLL_1 = 16, 8, 16

bf16 = jnp.bfloat16
f32 = jnp.float32


def _nt_dot(a, b):
    return lax.dot_general(a, b, (((1,), (1,)), ((), ())), preferred_element_type=f32)


def _sigmoid(x):
    return 1.0 / (1.0 + jnp.exp(-x))


def _proj_kernel(x_ref, pos_ref, ng_ref, w_ref, gains_ref, ones_ref, invf_ref,
                 *rest, chunks, rope_half, rope_group, has_f, res_rows, w_rows):
    if has_f:
        wf_ref, bf_ref = rest[0], rest[1]
        out_refs = rest[2:]
    else:
        out_refs = rest
    if res_rows is None:
        x = x_ref[...]
    else:
        x = jnp.concatenate([x_ref[0, :, rr, :] for rr in range(8)], axis=0)
    ms = jnp.mean(x * x, axis=-1, keepdims=True)
    hn = (x * lax.rsqrt(ms + RMS_EPS) * ng_ref[...]).astype(bf16)

    tm = x.shape[0]
    ang = invf_ref[...] * pos_ref[...].reshape(1, tm)
    cos_c, sin_c = jnp.cos(ang), jnp.sin(ang)
    zeros_c = jnp.zeros_like(ang)
    rest_rows = rope_group - 2 * rope_half
    reps = LANES // rope_group

    def lane_table(first, second, fill):
        blk = [first, second, jnp.full((rest_rows, tm), fill, f32)]
        return jnp.concatenate(blk * reps, axis=0).T

    cosv = lane_table(cos_c, cos_c, 1.0)
    sin_hi = lane_table(zeros_c, sin_c, 0.0)
    sin_lo = lane_table(-sin_c, zeros_c, 0.0)

    def put_head(o_ref, head, yb):
        if res_rows is None:
            o_ref[0, head] = yb
        else:
            for rr in range(8):
                o_ref[0, head, rr] = yb[rr * res_rows:(rr + 1) * res_rows]

    def put_cols(o_ref, col, yb):
        if res_rows is None:
            o_ref[:, col:col + LANES] = yb
        else:
            for rr in range(8):
                o_ref[0, rr, :, col:col + LANES] = yb[rr * res_rows:(rr + 1) * res_rows]

    for c, (kind, group, oi, slot) in enumerate(chunks):
        if w_rows:
            acc = _nt_dot(hn, w_ref[c * CHUNK:(c + 1) * CHUNK, :])
        else:
            acc = jnp.dot(hn, w_ref[:, c * CHUNK:(c + 1) * CHUNK], preferred_element_type=f32)
        if kind in ("rope", "norm") and group < LANES:
            ss = jnp.dot((acc * acc).astype(bf16), ones_ref[...], preferred_element_type=f32)
            acc = acc * lax.rsqrt(ss * (1.0 / group) + RMS_EPS) * gains_ref[c:c + 1, :]
        o_ref = out_refs[oi]
        for hh in range(CHUNK // LANES):
            yb = acc[:, hh * LANES:(hh + 1) * LANES]
            if kind in ("rope", "norm") and group == LANES:
                ms = jnp.mean(yb * yb, axis=-1, keepdims=True)
                yb = yb * lax.rsqrt(ms + RMS_EPS) * gains_ref[c:c + 1, hh * LANES:(hh + 1) * LANES]
            if kind == "rope":
                yb = (yb * cosv + pltpu.roll(yb, rope_half, 1) * sin_hi
                      + pltpu.roll(yb, LANES - rope_half, 1) * sin_lo)
            if kind == "plain_t":
                o_ref[0, slot * (CHUNK // LANES) + hh] = yb.T.astype(bf16)
            elif kind == "z":
                put_cols(o_ref, slot * CHUNK + hh * LANES, yb.astype(bf16))
            else:
                put_head(o_ref, slot * (CHUNK // LANES) + hh, yb.astype(bf16))

    if has_f:
        f = _nt_dot(hn, wf_ref[...]) + bf_ref[...]
        out_refs[-1][...] = jnp.minimum(f, 0.0) - jnp.log1p(jnp.exp(-jnp.abs(f)))


def _inv_freq_col(half):
    return jnp.exp(-math.log(ROPE_THETA) * jnp.arange(half, dtype=f32) / half).reshape(half, 1)


def _block_ones(group):
    i = np.arange(CHUNK)
    return jnp.asarray((i[:, None] // group == i[None, :] // group).astype(np.float32), dtype=bf16)


def _const_spec(shape):
    nd = len(shape)
    return pl.BlockSpec(shape, lambda *_: (0,) * nd)


def _proj_even(x2d, pos_col, ng, w_main, gains, w_f, b_f, tm):
    n_t = (BATCH * SEQ) // tm
    per_b = SEQ // tm
    chunks = (("rope", 64, 0, 0), ("rope", 64, 1, 0), ("plain_t", 0, 2, 0),
              ("norm", 128, 3, 0), ("norm", 128, 4, 0), ("plain_t", 0, 5, 0),
              ("z", 0, 6, 0), ("z", 0, 6, 1))
    head_shape = jax.ShapeDtypeStruct((BATCH, 4, SEQ, LANES), bf16)
    head_spec = pl.BlockSpec((1, 4, tm, LANES), lambda i: (i // per_b, 0, i % per_b, 0))
    head_t_shape = jax.ShapeDtypeStruct((BATCH, 4, LANES, SEQ), bf16)
    head_t_spec = pl.BlockSpec((1, 4, LANES, tm), lambda i: (i // per_b, 0, 0, i % per_b))
    row_spec = lambda w: pl.BlockSpec((tm, w), lambda i: (i, 0))
    kern = functools.partial(_proj_kernel, chunks=chunks, rope_half=8, rope_group=64, has_f=True,
                             res_rows=None, w_rows=True)
    return pl.pallas_call(
        kern,
        grid=(n_t,),
        in_specs=[row_spec(D_MODEL), pl.BlockSpec((1, 1, tm), lambda i: (i, 0, 0)),
                  _const_spec((1, D_MODEL)),
                  _const_spec(w_main.shape), _const_spec((8, CHUNK)),
                  _const_spec((CHUNK, CHUNK)),
                  _const_spec((8, 1)), _const_spec((LANES, D_MODEL)), _const_spec((1, LANES))],
        out_specs=[head_spec, head_spec, head_t_spec] * 2 + [row_spec(D_MODEL), row_spec(LANES)],
        out_shape=[head_shape, head_shape, head_t_shape] * 2 + [
            jax.ShapeDtypeStruct((BATCH * SEQ, D_MODEL), bf16),
            jax.ShapeDtypeStruct((BATCH * SEQ, LANES), f32)],
        compiler_params=pltpu.CompilerParams(dimension_semantics=("arbitrary",),
                                             vmem_limit_bytes=VMEM_LIMIT),
        name="proj_even",
    )(x2d, pos_col, ng, w_main, gains, _block_ones(64), _inv_freq_col(8),
      w_f, b_f)


def _proj_odd(h_res, pos_res, ng, w_main, gains):
    chunks = (("rope", 128, 0, 0), ("rope", 128, 0, 1), ("rope", 128, 1, 0), ("rope", 128, 1, 1),
              ("plain", 0, 2, 0), ("plain", 0, 2, 1), ("z", 0, 3, 0), ("z", 0, 3, 1))
    head_shape = jax.ShapeDtypeStruct((BATCH, 8, N_RES, SLAB, LANES), bf16)
    head_spec = pl.BlockSpec((1, 8, 8, RES_ROWS, LANES), lambda b, g, c: (b, 0, g, c, 0))
    kern = functools.partial(_proj_kernel, chunks=chunks, rope_half=16, rope_group=128, has_f=False,
                             res_rows=RES_ROWS, w_rows=False)
    n_g, n_c = N_RES // 8, SLAB // RES_ROWS
    return pl.pallas_call(
        kern,
        grid=(BATCH, n_g, n_c),
        in_specs=[pl.BlockSpec((1, RES_ROWS, 8, D_MODEL), lambda b, g, c: (b, c, g, 0)),
                  pl.BlockSpec((1, 1, 8 * RES_ROWS), lambda b, g, c: ((b * n_g + g) * n_c + c, 0, 0)),
                  _const_spec((1, D_MODEL)), _const_spec(w_main.shape), _const_spec((8, CHUNK)),
                  _const_spec((CHUNK, CHUNK)),
                  _const_spec((16, 1))],
        out_specs=[head_spec] * 3 + [pl.BlockSpec((1, 8, RES_ROWS, D_MODEL), lambda b, g, c: (b, g, c, 0))],
        out_shape=[head_shape] * 3 + [jax.ShapeDtypeStruct((BATCH, N_RES, SLAB, D_MODEL), bf16)],
        compiler_params=pltpu.CompilerParams(dimension_semantics=("arbitrary",) * 3,
                                             vmem_limit_bytes=VMEM_LIMIT),
        name="proj_odd",
    )(h_res, pos_res, ng, w_main, gains, _block_ones(64), _inv_freq_col(16))


def _split3(x):
    hi = x.astype(bf16).astype(f32)
    r1 = x - hi
    mid = r1.astype(bf16).astype(f32)
    lo = (r1 - mid).astype(bf16).astype(f32)
    return hi, mid, lo


def _split3_lanes(x):
    return jnp.concatenate(_split3(x), axis=1).astype(bf16)


def _cumsum_kernel(logf_ref, place_ref, aug_ref, *, blk):
    row = lax.broadcasted_iota(jnp.int32, (blk, blk), 0)
    col = lax.broadcasted_iota(jnp.int32, (blk, blk), 1)
    lower = (col <= row).astype(bf16)
    carry = jnp.zeros((1, LANES), f32)
    for i in range(SEQ // blk):
        x = logf_ref[i * blk:(i + 1) * blk, :]
        cs3 = jnp.dot(lower, _split3_lanes(x), preferred_element_type=f32)
        cs = (cs3[:, 0:LANES] + cs3[:, LANES:2 * LANES]) + cs3[:, 2 * LANES:3 * LANES] + carry
        carry = cs[blk - 1:blk, :]
        placed = jnp.dot(_split3_lanes(cs * (-LOG2E)), place_ref[...], preferred_element_type=f32)
        for hh in range(4):
            aug_ref[0, hh, i * blk:(i + 1) * blk, :] = placed[:, hh * LANES:(hh + 1) * LANES].astype(bf16)


def _place_matrices():
    pm = np.zeros((3 * LANES, 4 * LANES), np.float32)
    for k in range(3):
        for hh in range(4):
            pm[k * LANES + hh, hh * LANES + k] = 1.0
    return jnp.asarray(pm, dtype=bf16)


def _forget_cumsum(logf):
    return pl.pallas_call(
        functools.partial(_cumsum_kernel, blk=512),
        grid=(BATCH,),
        in_specs=[pl.BlockSpec((SEQ, LANES), lambda b: (b, 0)), _const_spec((3 * LANES, 4 * LANES))],
        out_specs=pl.BlockSpec((1, 4, SEQ, LANES), lambda b: (b, 0, 0, 0)),
        out_shape=jax.ShapeDtypeStruct((BATCH, 4, SEQ, LANES), bf16),
        compiler_params=pltpu.CompilerParams(dimension_semantics=("arbitrary",),
                                             vmem_limit_bytes=VMEM_LIMIT),
        name="forget_cumsum",
    )(logf, _place_matrices())


def _flash_sweep(qq, k_ref, vt_ref, aug_ref, acc_sc, m_sc, i, t):
    n = qq.shape[0]
    m_sc[...] = jnp.full(m_sc.shape, -jnp.inf, f32)
    acc_sc[...] = jnp.zeros(acc_sc.shape, f32)

    def scores(j):
        k = k_ref[0, 0, pl.ds(j * t, t), :]
        if aug_ref is not None:
            k = jnp.concatenate([k, aug_ref[0, 0, pl.ds(j * t, t), :]], axis=1)
        return _nt_dot(k, qq)

    def update(s, j, diag):
        if diag:
            key = lax.broadcasted_iota(jnp.int32, s.shape, 0)
            qry = lax.broadcasted_iota(jnp.int32, s.shape, 1)
            if n > t:
                qry = jnp.where(qry >= t, qry - t, qry)
            s = jnp.where(key <= qry, s, -jnp.inf)
        vt = vt_ref[0, 0, :, pl.ds(j * t, t)]
        vt_ext = jnp.concatenate([vt, jnp.ones((SUM_ROWS, t), bf16)], axis=0)
        m_prev = m_sc[...]
        m_new = jnp.maximum(m_prev, jnp.max(s, axis=0, keepdims=True))
        alpha = jnp.exp2(m_prev - m_new)
        p = jnp.exp2(s - m_new).astype(bf16)
        acc_sc[...] = alpha * acc_sc[...] + jnp.dot(vt_ext, p, preferred_element_type=f32)
        m_sc[...] = m_new

    def group(j0, size, last_diag):
        ss = [scores(j0 + d) for d in range(size)]
        for d in range(size):
            update(ss[d], j0 + d, last_diag and d == size - 1)

    for g in range(i // FLASH_GROUP):
        group(FLASH_GROUP * g, FLASH_GROUP, False)
    rem = i % FLASH_GROUP
    group(i - rem, rem + 1, True)


def _diff_attn_kernel(q_ref, k_ref, vt_ref, lam_ref, g_ref, o_ref, acc_sc, m_sc, *,
                      tq, lam_init):
    lam = (jnp.exp(jnp.sum(lam_ref[0:1, :] * lam_ref[1:2, :], axis=1, keepdims=True))
           - jnp.exp(jnp.sum(lam_ref[2:3, :] * lam_ref[3:4, :], axis=1, keepdims=True)) + lam_init)
    lane = lax.broadcasted_iota(jnp.int32, (tq, LANES), 1)

    for i in range(SEQ // tq):
        q = q_ref[0, 0, i * tq:(i + 1) * tq, :]
        zero = jnp.zeros_like(q)
        qq = jnp.concatenate([jnp.where(lane < 64, q, zero), jnp.where(lane >= 64, q, zero)], axis=0)
        _flash_sweep(qq, k_ref, vt_ref, None, acc_sc.at[i], m_sc.at[i], i, tq)
        ot = acc_sc[i, 0:LANES, :] / acc_sc[i, LANES:LANES + 1, :]
        ot = ot[:, 0:tq] - lam * ot[:, tq:2 * tq]
        ms = jnp.mean(ot * ot, axis=0, keepdims=True)
        ot = ot * lax.rsqrt(ms + RMS_EPS) * g_ref[...] * (1.0 - lam_init)
        o_ref[0, 0, i * tq:(i + 1) * tq, :] = ot.T.astype(bf16)


def _fox_attn_kernel(q_ref, k_ref, vt_ref, aug_ref, o_ref, acc_sc, m_sc, *, tq):
    lane = lax.broadcasted_iota(jnp.int32, (tq, LANES), 1)
    ones3 = jnp.where(lane < 3, 1.0, 0.0).astype(bf16)

    for i in range(SEQ // tq):
        qq = jnp.concatenate([q_ref[0, 0, i * tq:(i + 1) * tq, :], ones3], axis=1)
        _flash_sweep(qq, k_ref, vt_ref, aug_ref, acc_sc.at[i], m_sc.at[i], i, tq)
        o_ref[0, 0, i * tq:(i + 1) * tq, :] = (
            acc_sc[i, 0:LANES, :] / acc_sc[i, LANES:LANES + 1, :]).T.astype(bf16)

def _layer0_attention(q, k, vt, extra, extra_specs, kern, n_stack, tq, name):
    head_spec = pl.BlockSpec((1, 1, SEQ, LANES), lambda b, h: (b, h, 0, 0))
    vt_spec = pl.BlockSpec((1, 1, LANES, SEQ), lambda b, h: (b, h, 0, 0))
    n = n_stack * tq
    return pl.pallas_call(
        kern,
        grid=(BATCH, 4),
        in_specs=[head_spec, head_spec, vt_spec] + extra_specs,
        out_specs=head_spec,
        out_shape=jax.ShapeDtypeStruct((BATCH, 4, SEQ, LANES), bf16),
        scratch_shapes=[pltpu.VMEM((SEQ // tq, LANES + SUM_ROWS, n), f32),
                        pltpu.VMEM((SEQ // tq, 1, n), f32)],
        compiler_params=pltpu.CompilerParams(
            dimension_semantics=("arbitrary", "arbitrary"), vmem_limit_bytes=VMEM_LIMIT),
        name=name,
    )(q, k, vt, *extra)


def _pattern_bias(dil):
    n_c = N_RES // dil
    bq = max(128 // n_c, 16)
    bk = 2 * bq
    cq, iq = np.divmod(np.arange(n_c * bq), bq)
    ck, jk = np.divmod(np.arange(n_c * bk), bk)
    out = []
    for shift in (0, bq):
        dist = n_c * (shift + iq[:, None] - jk[None, :]) + (cq[:, None] - ck[None, :])
        out.append(np.where((dist >= 0) & (dist <= 128), 0.0, NEG))
    return jnp.asarray(np.stack(out), dtype=f32), n_c, bq


def _dilated_kernel(q_ref, k_ref, v_ref, b16_ref, b4_ref, b1_ref, o_ref, acc_sc, m_sc, l_sc, *, geo):
    def pieces(ref, lead, slabs, start, n):
        parts = [ref[lead + (s, pl.ds(start, n), slice(None))] for s in slabs]
        return parts[0] if len(parts) == 1 else jnp.concatenate(parts, axis=0)

    def load(slabs, ub, bq, bias_ref, first):
        u0 = ub * bq
        ks = jnp.maximum(ub - 1, 0) * bq
        t = dict(slabs=slabs, u0=u0, bq=bq, first=first,
                 q=pieces(q_ref, (0, 0), slabs, u0, bq),
                 k=pieces(k_ref, (0, 0), slabs, ks, 2 * bq),
                 v=pieces(v_ref, (0, 0), slabs, ks, 2 * bq),
                 bias=bias_ref[jnp.minimum(ub, 1)])
        if not first:
            t["m"] = pieces(m_sc, (), slabs, u0, bq)
            t["l"] = pieces(l_sc, (), slabs, u0, bq)
            t["a"] = pieces(acc_sc, (), slabs, u0, bq)
        return t

    def compute(t):
        s = _nt_dot(t["q"], t["k"]) + t["bias"]
        m_cur = jnp.max(s, axis=1, keepdims=True)
        v_ext = jnp.concatenate([t["v"], jnp.ones(t["v"].shape, bf16)], axis=1)
        if t["first"]:
            m_new = jnp.broadcast_to(m_cur, (s.shape[0], LANES))
        else:
            m_new = jnp.maximum(t["m"], m_cur)
            alpha = jnp.exp2(t["m"] - m_new)
        p = jnp.exp2(s - jnp.concatenate([m_new] * (s.shape[1] // LANES), axis=1))
        ext = jnp.dot(p.astype(bf16), v_ext, preferred_element_type=f32)
        a_new, l_new = ext[:, :LANES], ext[:, LANES:]
        if not t["first"]:
            l_new = l_new + alpha * t["l"]
            a_new = a_new + alpha * t["a"]
        return m_new, l_new, a_new

    def store(t, res):
        bq, u0 = t["bq"], t["u0"]
        for n, s_id in enumerate(t["slabs"]):
            rows = slice(n * bq, (n + 1) * bq)
            for ref, val in zip((m_sc, l_sc, acc_sc), res):
                ref[s_id, pl.ds(u0, bq), :] = val[rows]

    def run(tiles):
        results = [compute(t) for t in tiles]
        for t, res in zip(tiles, results):
            store(t, res)

    first = True
    for (dil, n_c, bq), bias_ref in zip(geo, (b16_ref, b4_ref, b1_ref)):
        n_tiles = SLAB // bq
        if n_c == 1:
            def body(rg, carry, bq=bq, bias_ref=bias_ref, first=first, n_tiles=n_tiles):
                run([load([UNROLL_16 * rg + dr], ub, bq, bias_ref, first)
                     for dr in range(UNROLL_16) for ub in range(n_tiles)])
                return carry
            lax.fori_loop(0, N_RES // UNROLL_16, body, 0)
        elif n_c == N_RES:
            def body(pp, carry, bq=bq, bias_ref=bias_ref, first=first):
                run([load(list(range(N_RES)), UNROLL_1 * pp + du, bq, bias_ref, first)
                     for du in range(UNROLL_1)])
                return carry
            lax.fori_loop(0, n_tiles // UNROLL_1, body, 0)
        else:
            def body(up, carry, dil=dil, n_c=n_c, bq=bq, bias_ref=bias_ref, first=first):
                run([load([c * dil + res for c in range(n_c)], UNROLL_4 * up + du, bq, bias_ref, first)
                     for du in range(UNROLL_4) for res in range(dil)])
                return carry
            lax.fori_loop(0, n_tiles // UNROLL_4, body, 0)
        first = False

    def fin(r, carry):
        o_ref[0, 0, r] = (acc_sc[r] / l_sc[r]).astype(bf16)
        return carry
    lax.fori_loop(0, N_RES, fin, 0)


def _dilated_attention(q, k, v):
    biases, geo = [], []
    for dil in (16, 4, 1):
        b, n_c, bq = _pattern_bias(dil)
        biases.append(b)
        geo.append((dil, n_c, bq))
    spec = pl.BlockSpec((1, 1, N_RES, SLAB, LANES), lambda b, h: (b, h, 0, 0, 0))
    return pl.pallas_call(
        functools.partial(_dilated_kernel, geo=tuple(geo)),
        grid=(BATCH, 8),
        in_specs=[spec, spec, spec] + [_const_spec(b.shape) for b in biases],
        out_specs=spec,
        out_shape=jax.ShapeDtypeStruct((BATCH, 8, N_RES, SLAB, LANES), bf16),
        scratch_shapes=[pltpu.VMEM((N_RES, SLAB, LANES), f32)] * 3,
        compiler_params=pltpu.CompilerParams(dimension_semantics=("arbitrary", "arbitrary"),
                                             vmem_limit_bytes=VMEM_LIMIT),
        name="dilated_attn",
    )(q, k, v, *biases)


def _post_kernel(h_ref, *rest, n_o, res_rows):
    o_refs = rest[:n_o]
    z_ref, p_ref, wo_ref, wg_ref, wp_ref, out_ref, mixed_sc, h1_sc = rest[n_o:]
    heads_per = D_MODEL // LANES // n_o
    for n, o_ref in enumerate(o_refs):
        for hh in range(heads_per):
            col = (n * heads_per + hh) * LANES
            if res_rows is None:
                o = o_ref[0, hh].astype(f32)
                z = z_ref[:, col:col + LANES].astype(f32)
                mixed_sc[:, col:col + LANES] = (o * (z * _sigmoid(z))).astype(bf16)
            else:
                for rr in range(8):
                    o = o_ref[0, hh, rr].astype(f32)
                    z = z_ref[0, rr, :, col:col + LANES].astype(f32)
                    mixed_sc[rr * res_rows:(rr + 1) * res_rows, col:col + LANES] = (
                        o * (z * _sigmoid(z))).astype(bf16)
    if res_rows is None:
        h, pb = h_ref[...], p_ref[...]
    else:
        h = jnp.concatenate([h_ref[0, :, rr, :] for rr in range(8)], axis=0)
        pb = jnp.concatenate([p_ref[0, :, rr, :] for rr in range(8)], axis=0)
    pb = pb.astype(bf16)
    h1 = h + jnp.dot(mixed_sc[...], wo_ref[...], preferred_element_type=f32)
    h1_sc[...] = h1
    h1b = h1.astype(bf16)
    for c in range(D_MODEL // CHUNK):
        cols = slice(c * CHUNK, (c + 1) * CHUNK)
        gate = _sigmoid(jnp.dot(h1b, wg_ref[:, cols], preferred_element_type=f32))
        emb = jnp.dot(pb, wp_ref[:, cols], preferred_element_type=f32)
        res = h1_sc[:, cols] + emb * gate
        if res_rows is None:
            out_ref[:, cols] = res
        else:
            for rr in range(8):
                out_ref[0, :, rr, cols] = res[rr * res_rows:(rr + 1) * res_rows]


def _post_even(x2d, oa, ob, z, p0, wo, wg, wp, tm):
    n_t = (BATCH * SEQ) // tm
    per_b = SEQ // tm
    row_spec = lambda w: pl.BlockSpec((tm, w), lambda i: (i, 0))
    head_spec = pl.BlockSpec((1, 4, tm, LANES), lambda i: (i // per_b, 0, i % per_b, 0))
    return pl.pallas_call(
        functools.partial(_post_kernel, n_o=2, res_rows=None),
        grid=(n_t,),
        in_specs=[row_spec(D_MODEL), head_spec, head_spec, row_spec(D_MODEL),
                  pl.BlockSpec((None, tm, PLE_DIM), lambda i: (0, i, 0)),
                  _const_spec(wo.shape), _const_spec(wg.shape), _const_spec(wp.shape)],
        out_specs=row_spec(D_MODEL),
        out_shape=jax.ShapeDtypeStruct((BATCH * SEQ, D_MODEL), f32),
        scratch_shapes=[pltpu.VMEM((tm, D_MODEL), bf16), pltpu.VMEM((tm, D_MODEL), f32)],
        compiler_params=pltpu.CompilerParams(dimension_semantics=("arbitrary",),
                                             vmem_limit_bytes=VMEM_LIMIT),
        name="post_even",
    )(x2d, oa, ob, z, p0, wo, wg, wp)


def _post_odd(h_res, o, z, p_res, wo, wg, wp):
    tm = 8 * RES_ROWS
    res_row = lambda w: pl.BlockSpec((1, RES_ROWS, 8, w), lambda b, g, c: (b, c, g, 0))
    return pl.pallas_call(
        functools.partial(_post_kernel, n_o=1, res_rows=RES_ROWS),
        grid=(BATCH, N_RES // 8, SLAB // RES_ROWS),
        in_specs=[res_row(D_MODEL),
                  pl.BlockSpec((1, 8, 8, RES_ROWS, LANES), lambda b, g, c: (b, 0, g, c, 0)),
                  pl.BlockSpec((1, 8, RES_ROWS, D_MODEL), lambda b, g, c: (b, g, c, 0)),
                  pl.BlockSpec((None, 1, RES_ROWS, 8, PLE_DIM), lambda b, g, c: (1, b, c, g, 0)),
                  _const_spec(wo.shape), _const_spec(wg.shape), _const_spec(wp.shape)],
        out_specs=res_row(D_MODEL),
        out_shape=jax.ShapeDtypeStruct((BATCH, SLAB, N_RES, D_MODEL), f32),
        scratch_shapes=[pltpu.VMEM((tm, D_MODEL), bf16), pltpu.VMEM((tm, D_MODEL), f32)],
        compiler_params=pltpu.CompilerParams(dimension_semantics=("arbitrary",) * 3,
                                             vmem_limit_bytes=VMEM_LIMIT),
        name="post_odd",
    )(h_res, o, z, p_res, wo, wg, wp)


def _tile4(g):
    return jnp.tile(g.astype(f32), CHUNK // g.shape[0])


def kernel(x, p, positions, norm_g, w_in_even, b_forget, qn_a, kn_a, qn_b, kn_b, lam_q1, lam_k1,
           lam_q2, lam_k2, subln_g, w_out_even, w_in_odd, qn_c, kn_c, w_out_odd, w_ple, w_ple_gate):
    tm = 512
    tq_diff, tq_fox = 512, 512
    n_tok = BATCH * SEQ
    x2d = x.reshape(n_tok, D_MODEL)
    pos_f = positions.astype(f32)

    w0 = w_in_even[0]
    f_lo, f_hi = 3072, 3076
    w0t = w0.T
    w_main0 = jnp.concatenate([w0t[:f_lo], w0t[f_hi:]], axis=0).astype(bf16)
    w_f = jnp.pad(w0t[f_lo:f_hi], ((0, LANES - 4), (0, 0))).astype(bf16)
    b_f = jnp.pad(b_forget[0].astype(f32), (0, LANES - 4)).reshape(1, LANES)
    ones_row = jnp.ones((CHUNK,), f32)
    gains0 = jnp.stack([_tile4(qn_a[0]) * (64 ** -0.5 * LOG2E), _tile4(kn_a[0]), ones_row,
                        _tile4(qn_b[0]) * (HEAD_DIM ** -0.5 * LOG2E), _tile4(kn_b[0]), ones_row,
                        ones_row, ones_row])
    qa, ka, va, qb, kb, vb, z0, logf = _proj_even(
        x2d, pos_f.reshape(n_tok // tm, 1, tm), norm_g[0].reshape(1, D_MODEL).astype(f32), w_main0, gains0,
        w_f, b_f, tm)
    cum = _forget_cumsum(logf)

    lam_rows = jnp.pad(jnp.stack([lam_q1[0], lam_k1[0], lam_q2[0], lam_k2[0]]).astype(f32),
                       ((0, 4), (0, LANES - 64)))
    lam_init = 0.8 - 0.6 * math.exp(-0.3 * 0)
    oa = _layer0_attention(
        qa, ka, va, [lam_rows, subln_g[0].reshape(LANES, 1).astype(f32)],
        [_const_spec((8, LANES)), _const_spec((LANES, 1))],
        functools.partial(_diff_attn_kernel, tq=tq_diff, lam_init=lam_init),
        2, tq_diff, "diff_attn")
    ob = _layer0_attention(
        qb, kb, vb, [cum], [pl.BlockSpec((1, 1, SEQ, LANES), lambda b, h: (b, h, 0, 0))],
        functools.partial(_fox_attn_kernel, tq=tq_fox), 1, tq_fox, "fox_attn")
    h1 = _post_even(x2d, oa, ob, z0, p.reshape(2, n_tok, PLE_DIM), w_out_even[0].astype(bf16),
                    w_ple_gate[0].astype(bf16), w_ple[0].astype(bf16), tm)

    h_res = h1.reshape(BATCH, SLAB, N_RES, D_MODEL)
    pos_res = (pos_f.reshape(BATCH, SLAB // RES_ROWS, RES_ROWS, N_RES // 8, 8)
               .transpose(0, 3, 1, 4, 2).reshape(-1, 1, 8 * RES_ROWS))
    ones_row2 = jnp.ones((CHUNK,), f32)
    gq = _tile4(qn_c[0]) * (HEAD_DIM ** -0.5 * LOG2E)
    gk = _tile4(kn_c[0])
    gains1 = jnp.stack([gq, gq, gk, gk, ones_row2, ones_row2, ones_row2, ones_row2])
    q1, k1, v1, z1 = _proj_odd(h_res, pos_res, norm_g[1].reshape(1, D_MODEL).astype(f32),
                               w_in_odd[0].astype(bf16), gains1)
    o1 = _dilated_attention(q1, k1, v1)
    out = _post_odd(h_res, o1, z1, p.reshape(2, BATCH, SLAB, N_RES, PLE_DIM),
                    w_out_odd[0].astype(bf16), w_ple_gate[1].astype(bf16), w_ple[1].astype(bf16))
    return out.reshape(BATCH, SEQ, D_MODEL)
```

```python
import functools
import math

import numpy as np
import jax
import jax.numpy as jnp
from jax import lax
from jax.experimental import pallas as pl
from jax.experimental.pallas import tpu as pltpu

D_MODEL = 1024
BATCH = 8
SEQ = 4096
PLE_DIM = 256
HEAD_DIM = 128
ROPE_THETA = 500000.0
RMS_EPS = 1e-6
LANES = 128
CHUNK = 512
N_RES = 16
SLAB = SEQ // N_RES
RES_ROWS = 64
VMEM_LIMIT = 56 * 1024 * 1024
NEG = -1e30
LOG2E = math.log2(math.e)
FLASH_GROUP = 8
SUM_ROWS = 16
UNROLL_16, UNROLL_4, UNROLL_1 = 16, 8, 16
TOKEN_TILE = 512
ATTN_TILE = 512

bf16 = jnp.bfloat16
f32 = jnp.float32


def _nt_dot(a, b):
    return lax.dot_general(a, b, (((1,), (1,)), ((), ())), preferred_element_type=f32)


def _sigmoid(x):
    return 1.0 / (1.0 + jnp.exp(-x))


def _proj_kernel(x_ref, pos_ref, ng_ref, w_ref, gains_ref, ones_ref, invf_ref,
                 *rest, chunks, rope_half, rope_group, has_f, res_rows, w_rows):
    if has_f:
        wf_ref, bf_ref = rest[0], rest[1]
        out_refs = rest[2:]
    else:
        out_refs = rest
    if res_rows is None:
        x = x_ref[...]
    else:
        x = jnp.concatenate([x_ref[0, :, rr, :] for rr in range(8)], axis=0)
    ms = jnp.mean(x * x, axis=-1, keepdims=True)
    hn = (x * lax.rsqrt(ms + RMS_EPS) * ng_ref[...]).astype(bf16)

    tm = x.shape[0]
    ang = invf_ref[...] * pos_ref[...].reshape(1, tm)
    cos_c, sin_c = jnp.cos(ang), jnp.sin(ang)
    zeros_c = jnp.zeros_like(ang)
    rest_rows = rope_group - 2 * rope_half
    reps = LANES // rope_group

    def lane_table(first, second, fill):
        blk = [first, second, jnp.full((rest_rows, tm), fill, f32)]
        return jnp.concatenate(blk * reps, axis=0).T

    cosv = lane_table(cos_c, cos_c, 1.0)
    sin_hi = lane_table(zeros_c, sin_c, 0.0)
    sin_lo = lane_table(-sin_c, zeros_c, 0.0)

    def put_head(o_ref, head, yb):
        if res_rows is None:
            o_ref[0, head] = yb
        else:
            for rr in range(8):
                o_ref[0, head, rr] = yb[rr * res_rows:(rr + 1) * res_rows]

    def put_cols(o_ref, col, yb):
        if res_rows is None:
            o_ref[:, col:col + LANES] = yb
        else:
            for rr in range(8):
                o_ref[0, rr, :, col:col + LANES] = yb[rr * res_rows:(rr + 1) * res_rows]

    for c, (kind, group, oi, slot) in enumerate(chunks):
        if w_rows:
            acc = _nt_dot(hn, w_ref[c * CHUNK:(c + 1) * CHUNK, :])
        else:
            acc = jnp.dot(hn, w_ref[:, c * CHUNK:(c + 1) * CHUNK], preferred_element_type=f32)
        if kind in ("rope", "norm") and group < LANES:
            ss = jnp.dot((acc * acc).astype(bf16), ones_ref[...], preferred_element_type=f32)
            acc = acc * lax.rsqrt(ss * (1.0 / group) + RMS_EPS) * gains_ref[c:c + 1, :]
        o_ref = out_refs[oi]
        for hh in range(CHUNK // LANES):
            yb = acc[:, hh * LANES:(hh + 1) * LANES]
            if kind in ("rope", "norm") and group == LANES:
                ms = jnp.mean(yb * yb, axis=-1, keepdims=True)
                yb = yb * lax.rsqrt(ms + RMS_EPS) * gains_ref[c:c + 1, hh * LANES:(hh + 1) * LANES]
            if kind == "rope":
                yb = (yb * cosv + pltpu.roll(yb, rope_half, 1) * sin_hi
                      + pltpu.roll(yb, LANES - rope_half, 1) * sin_lo)
            if kind == "plain_t":
                o_ref[0, slot * (CHUNK // LANES) + hh] = yb.T.astype(bf16)
            elif kind == "z":
                put_cols(o_ref, slot * CHUNK + hh * LANES, yb.astype(bf16))
            else:
                put_head(o_ref, slot * (CHUNK // LANES) + hh, yb.astype(bf16))

    if has_f:
        f = _nt_dot(hn, wf_ref[...]) + bf_ref[...]
        out_refs[-1][...] = jnp.minimum(f, 0.0) - jnp.log1p(jnp.exp(-jnp.abs(f)))


def _inv_freq_col(half):
    return jnp.exp(-math.log(ROPE_THETA) * jnp.arange(half, dtype=f32) / half).reshape(half, 1)


def _block_ones(group):
    i = np.arange(CHUNK)
    return jnp.asarray((i[:, None] // group == i[None, :] // group).astype(np.float32), dtype=bf16)


def _const_spec(shape):
    nd = len(shape)
    return pl.BlockSpec(shape, lambda *_: (0,) * nd)


def _proj_even(x2d, pos_col, ng, w_main, gains, w_f, b_f, tm):
    n_t = (BATCH * SEQ) // tm
    per_b = SEQ // tm
    chunks = (("rope", 64, 0, 0), ("rope", 64, 1, 0), ("plain_t", 0, 2, 0),
              ("norm", 128, 3, 0), ("norm", 128, 4, 0), ("plain_t", 0, 5, 0),
              ("z", 0, 6, 0), ("z", 0, 6, 1))
    head_shape = jax.ShapeDtypeStruct((BATCH, 4, SEQ, LANES), bf16)
    head_spec = pl.BlockSpec((1, 4, tm, LANES), lambda i: (i // per_b, 0, i % per_b, 0))
    head_t_shape = jax.ShapeDtypeStruct((BATCH, 4, LANES, SEQ), bf16)
    head_t_spec = pl.BlockSpec((1, 4, LANES, tm), lambda i: (i // per_b, 0, 0, i % per_b))
    row_spec = lambda w: pl.BlockSpec((tm, w), lambda i: (i, 0))
    kern = functools.partial(_proj_kernel, chunks=chunks, rope_half=8, rope_group=64, has_f=True,
                             res_rows=None, w_rows=True)
    return pl.pallas_call(
        kern,
        grid=(n_t,),
        in_specs=[row_spec(D_MODEL), pl.BlockSpec((1, 1, tm), lambda i: (i, 0, 0)),
                  _const_spec((1, D_MODEL)),
                  _const_spec(w_main.shape), _const_spec((8, CHUNK)),
                  _const_spec((CHUNK, CHUNK)),
                  _const_spec((8, 1)), _const_spec((LANES, D_MODEL)), _const_spec((1, LANES))],
        out_specs=[head_spec, head_spec, head_t_spec] * 2 + [row_spec(D_MODEL), row_spec(LANES)],
        out_shape=[head_shape, head_shape, head_t_shape] * 2 + [
            jax.ShapeDtypeStruct((BATCH * SEQ, D_MODEL), bf16),
            jax.ShapeDtypeStruct((BATCH * SEQ, LANES), f32)],
        compiler_params=pltpu.CompilerParams(dimension_semantics=("arbitrary",),
                                             vmem_limit_bytes=VMEM_LIMIT),
        name="proj_even",
    )(x2d, pos_col, ng, w_main, gains, _block_ones(64), _inv_freq_col(8),
      w_f, b_f)


def _proj_odd(h_res, pos_res, ng, w_main, gains):
    chunks = (("rope", 128, 0, 0), ("rope", 128, 0, 1), ("rope", 128, 1, 0), ("rope", 128, 1, 1),
              ("plain", 0, 2, 0), ("plain", 0, 2, 1), ("z", 0, 3, 0), ("z", 0, 3, 1))
    head_shape = jax.ShapeDtypeStruct((BATCH, 8, N_RES, SLAB, LANES), bf16)
    head_spec = pl.BlockSpec((1, 8, 8, RES_ROWS, LANES), lambda b, g, c: (b, 0, g, c, 0))
    kern = functools.partial(_proj_kernel, chunks=chunks, rope_half=16, rope_group=128, has_f=False,
                             res_rows=RES_ROWS, w_rows=False)
    n_g, n_c = N_RES // 8, SLAB // RES_ROWS
    return pl.pallas_call(
        kern,
        grid=(BATCH, n_g, n_c),
        in_specs=[pl.BlockSpec((1, RES_ROWS, 8, D_MODEL), lambda b, g, c: (b, c, g, 0)),
                  pl.BlockSpec((1, 1, 8 * RES_ROWS), lambda b, g, c: ((b * n_g + g) * n_c + c, 0, 0)),
                  _const_spec((1, D_MODEL)), _const_spec(w_main.shape), _const_spec((8, CHUNK)),
                  _const_spec((CHUNK, CHUNK)),
                  _const_spec((16, 1))],
        out_specs=[head_spec] * 3 + [pl.BlockSpec((1, 8, RES_ROWS, D_MODEL), lambda b, g, c: (b, g, c, 0))],
        out_shape=[head_shape] * 3 + [jax.ShapeDtypeStruct((BATCH, N_RES, SLAB, D_MODEL), bf16)],
        compiler_params=pltpu.CompilerParams(dimension_semantics=("arbitrary",) * 3,
                                             vmem_limit_bytes=VMEM_LIMIT),
        name="proj_odd",
    )(h_res, pos_res, ng, w_main, gains, _block_ones(64), _inv_freq_col(16))


def _split3(x):
    hi = x.astype(bf16).astype(f32)
    r1 = x - hi
    mid = r1.astype(bf16).astype(f32)
    lo = (r1 - mid).astype(bf16).astype(f32)
    return hi, mid, lo


def _split3_lanes(x):
    return jnp.concatenate(_split3(x), axis=1).astype(bf16)


def _cumsum_kernel(logf_ref, place_ref, aug_ref, *, blk):
    row = lax.broadcasted_iota(jnp.int32, (blk, blk), 0)
    col = lax.broadcasted_iota(jnp.int32, (blk, blk), 1)
    lower = (col <= row).astype(bf16)
    carry = jnp.zeros((1, LANES), f32)
    for i in range(SEQ // blk):
        x = logf_ref[i * blk:(i + 1) * blk, :]
        cs3 = jnp.dot(lower, _split3_lanes(x), preferred_element_type=f32)
        cs = (cs3[:, 0:LANES] + cs3[:, LANES:2 * LANES]) + cs3[:, 2 * LANES:3 * LANES] + carry
        carry = cs[blk - 1:blk, :]
        placed = jnp.dot(_split3_lanes(cs * (-LOG2E)), place_ref[...], preferred_element_type=f32)
        for hh in range(4):
            aug_ref[0, hh, i * blk:(i + 1) * blk, :] = placed[:, hh * LANES:(hh + 1) * LANES].astype(bf16)


def _place_matrices():
    pm = np.zeros((3 * LANES, 4 * LANES), np.float32)
    for k in range(3):
        for hh in range(4):
            pm[k * LANES + hh, hh * LANES + k] = 1.0
    return jnp.asarray(pm, dtype=bf16)


def _forget_cumsum(logf):
    return pl.pallas_call(
        functools.partial(_cumsum_kernel, blk=512),
        grid=(BATCH,),
        in_specs=[pl.BlockSpec((SEQ, LANES), lambda b: (b, 0)), _const_spec((3 * LANES, 4 * LANES))],
        out_specs=pl.BlockSpec((1, 4, SEQ, LANES), lambda b: (b, 0, 0, 0)),
        out_shape=jax.ShapeDtypeStruct((BATCH, 4, SEQ, LANES), bf16),
        compiler_params=pltpu.CompilerParams(dimension_semantics=("arbitrary",),
                                             vmem_limit_bytes=VMEM_LIMIT),
        name="forget_cumsum",
    )(logf, _place_matrices())


def _flash_sweep(qq, k_ref, vt_ref, aug_ref, acc_sc, m_sc, i, t):
    n = qq.shape[0]
    m_sc[...] = jnp.full(m_sc.shape, -jnp.inf, f32)
    acc_sc[...] = jnp.zeros(acc_sc.shape, f32)

    def scores(j):
        k = k_ref[0, 0, pl.ds(j * t, t), :]
        if aug_ref is not None:
            k = jnp.concatenate([k, aug_ref[0, 0, pl.ds(j * t, t), :]], axis=1)
        return _nt_dot(k, qq)

    def update(s, j, diag):
        if diag:
            key = lax.broadcasted_iota(jnp.int32, s.shape, 0)
            qry = lax.broadcasted_iota(jnp.int32, s.shape, 1)
            if n > t:
                qry = jnp.where(qry >= t, qry - t, qry)
            s = jnp.where(key <= qry, s, -jnp.inf)
        vt = vt_ref[0, 0, :, pl.ds(j * t, t)]
        vt_ext = jnp.concatenate([vt, jnp.ones((SUM_ROWS, t), bf16)], axis=0)
        m_prev = m_sc[...]
        m_new = jnp.maximum(m_prev, jnp.max(s, axis=0, keepdims=True))
        alpha = jnp.exp2(m_prev - m_new)
        p = jnp.exp2(s - m_new).astype(bf16)
        acc_sc[...] = alpha * acc_sc[...] + jnp.dot(vt_ext, p, preferred_element_type=f32)
        m_sc[...] = m_new

    def group(j0, size, last_diag):
        ss = [scores(j0 + d) for d in range(size)]
        for d in range(size):
            update(ss[d], j0 + d, last_diag and d == size - 1)

    for g in range(i // FLASH_GROUP):
        group(FLASH_GROUP * g, FLASH_GROUP, False)
    rem = i % FLASH_GROUP
    group(i - rem, rem + 1, True)


def _diff_attn_kernel(q_ref, k_ref, vt_ref, lam_ref, g_ref, o_ref, acc_sc, m_sc, *,
                      tq, lam_init):
    lam = (jnp.exp(jnp.sum(lam_ref[0:1, :] * lam_ref[1:2, :], axis=1, keepdims=True))
           - jnp.exp(jnp.sum(lam_ref[2:3, :] * lam_ref[3:4, :], axis=1, keepdims=True)) + lam_init)
    lane = lax.broadcasted_iota(jnp.int32, (tq, LANES), 1)

    for i in range(SEQ // tq):
        q = q_ref[0, 0, i * tq:(i + 1) * tq, :]
        zero = jnp.zeros_like(q)
        qq = jnp.concatenate([jnp.where(lane < 64, q, zero), jnp.where(lane >= 64, q, zero)], axis=0)
        _flash_sweep(qq, k_ref, vt_ref, None, acc_sc.at[i], m_sc.at[i], i, tq)
        ot = acc_sc[i, 0:LANES, :] / acc_sc[i, LANES:LANES + 1, :]
        ot = ot[:, 0:tq] - lam * ot[:, tq:2 * tq]
        ms = jnp.mean(ot * ot, axis=0, keepdims=True)
        ot = ot * lax.rsqrt(ms + RMS_EPS) * g_ref[...] * (1.0 - lam_init)
        o_ref[0, 0, i * tq:(i + 1) * tq, :] = ot.T.astype(bf16)


def _fox_attn_kernel(q_ref, k_ref, vt_ref, aug_ref, o_ref, acc_sc, m_sc, *, tq):
    lane = lax.broadcasted_iota(jnp.int32, (tq, LANES), 1)
    ones3 = jnp.where(lane < 3, 1.0, 0.0).astype(bf16)

    for i in range(SEQ // tq):
        qq = jnp.concatenate([q_ref[0, 0, i * tq:(i + 1) * tq, :], ones3], axis=1)
        _flash_sweep(qq, k_ref, vt_ref, aug_ref, acc_sc.at[i], m_sc.at[i], i, tq)
        o_ref[0, 0, i * tq:(i + 1) * tq, :] = (
            acc_sc[i, 0:LANES, :] / acc_sc[i, LANES:LANES + 1, :]).T.astype(bf16)


def _layer0_attention(q, k, vt, extra, extra_specs, kern, n_stack, tq, name):
    head_spec = pl.BlockSpec((1, 1, SEQ, LANES), lambda b, h: (b, h, 0, 0))
    vt_spec = pl.BlockSpec((1, 1, LANES, SEQ), lambda b, h: (b, h, 0, 0))
    n = n_stack * tq
    return pl.pallas_call(
        kern,
        grid=(BATCH, 4),
        in_specs=[head_spec, head_spec, vt_spec] + extra_specs,
        out_specs=head_spec,
        out_shape=jax.ShapeDtypeStruct((BATCH, 4, SEQ, LANES), bf16),
        scratch_shapes=[pltpu.VMEM((SEQ // tq, LANES + SUM_ROWS, n), f32),
                        pltpu.VMEM((SEQ // tq, 1, n), f32)],
        compiler_params=pltpu.CompilerParams(
            dimension_semantics=("arbitrary", "arbitrary"), vmem_limit_bytes=VMEM_LIMIT),
        name=name,
    )(q, k, vt, *extra)


def _pattern_bias(dil):
    n_c = N_RES // dil
    bq = max(128 // n_c, 16)
    bk = 2 * bq
    cq, iq = np.divmod(np.arange(n_c * bq), bq)
    ck, jk = np.divmod(np.arange(n_c * bk), bk)
    out = []
    for shift in (0, bq):
        dist = n_c * (shift + iq[:, None] - jk[None, :]) + (cq[:, None] - ck[None, :])
        out.append(np.where((dist >= 0) & (dist <= 128), 0.0, NEG))
    return jnp.asarray(np.stack(out), dtype=f32), n_c, bq


def _dilated_kernel(q_ref, k_ref, v_ref, b16_ref, b4_ref, b1_ref, o_ref, acc_sc, m_sc, l_sc, *, geo):
    def pieces(ref, lead, slabs, start, n):
        parts = [ref[lead + (s, pl.ds(start, n), slice(None))] for s in slabs]
        return parts[0] if len(parts) == 1 else jnp.concatenate(parts, axis=0)

    def load(slabs, ub, bq, bias_ref, first):
        u0 = ub * bq
        ks = jnp.maximum(ub - 1, 0) * bq
        t = dict(slabs=slabs, u0=u0, bq=bq, first=first,
                 q=pieces(q_ref, (0, 0), slabs, u0, bq),
                 k=pieces(k_ref, (0, 0), slabs, ks, 2 * bq),
                 v=pieces(v_ref, (0, 0), slabs, ks, 2 * bq),
                 bias=bias_ref[jnp.minimum(ub, 1)])
        if not first:
            t["m"] = pieces(m_sc, (), slabs, u0, bq)
            t["l"] = pieces(l_sc, (), slabs, u0, bq)
            t["a"] = pieces(acc_sc, (), slabs, u0, bq)
        return t

    def compute(t):
        s = _nt_dot(t["q"], t["k"]) + t["bias"]
        m_cur = jnp.max(s, axis=1, keepdims=True)
        v_ext = jnp.concatenate([t["v"], jnp.ones(t["v"].shape, bf16)], axis=1)
        if t["first"]:
            m_new = jnp.broadcast_to(m_cur, (s.shape[0], LANES))
        else:
            m_new = jnp.maximum(t["m"], m_cur)
            alpha = jnp.exp2(t["m"] - m_new)
        p = jnp.exp2(s - jnp.concatenate([m_new] * (s.shape[1] // LANES), axis=1))
        ext = jnp.dot(p.astype(bf16), v_ext, preferred_element_type=f32)
        a_new, l_new = ext[:, :LANES], ext[:, LANES:]
        if not t["first"]:
            l_new = l_new + alpha * t["l"]
            a_new = a_new + alpha * t["a"]
        return m_new, l_new, a_new

    def store(t, res):
        bq, u0 = t["bq"], t["u0"]
        for n, s_id in enumerate(t["slabs"]):
            rows = slice(n * bq, (n + 1) * bq)
            for ref, val in zip((m_sc, l_sc, acc_sc), res):
                ref[s_id, pl.ds(u0, bq), :] = val[rows]

    def run(tiles):
        results = [compute(t) for t in tiles]
        for t, res in zip(tiles, results):
            store(t, res)

    first = True
    for (dil, n_c, bq), bias_ref in zip(geo, (b16_ref, b4_ref, b1_ref)):
        n_tiles = SLAB // bq
        if n_c == 1:
            def body(rg, carry, bq=bq, bias_ref=bias_ref, first=first, n_tiles=n_tiles):
                run([load([UNROLL_16 * rg + dr], ub, bq, bias_ref, first)
                     for dr in range(UNROLL_16) for ub in range(n_tiles)])
                return carry
            lax.fori_loop(0, N_RES // UNROLL_16, body, 0)
        elif n_c == N_RES:
            def body(pp, carry, bq=bq, bias_ref=bias_ref, first=first):
                run([load(list(range(N_RES)), UNROLL_1 * pp + du, bq, bias_ref, first)
                     for du in range(UNROLL_1)])
                return carry
            lax.fori_loop(0, n_tiles // UNROLL_1, body, 0)
        else:
            def body(up, carry, dil=dil, n_c=n_c, bq=bq, bias_ref=bias_ref, first=first):
                run([load([c * dil + res for c in range(n_c)], UNROLL_4 * up + du, bq, bias_ref, first)
                     for du in range(UNROLL_4) for res in range(dil)])
                return carry
            lax.fori_loop(0, n_tiles // UNROLL_4, body, 0)
        first = False

    def fin(r, carry):
        o_ref[0, 0, r] = (acc_sc[r] / l_sc[r]).astype(bf16)
        return carry
    lax.fori_loop(0, N_RES, fin, 0)


def _dilated_attention(q, k, v):
    biases, geo = [], []
    for dil in (16, 4, 1):
        b, n_c, bq = _pattern_bias(dil)
        biases.append(b)
        geo.append((dil, n_c, bq))
    spec = pl.BlockSpec((1, 1, N_RES, SLAB, LANES), lambda b, h: (b, h, 0, 0, 0))
    return pl.pallas_call(
        functools.partial(_dilated_kernel, geo=tuple(geo)),
        grid=(BATCH, 8),
        in_specs=[spec, spec, spec] + [_const_spec(b.shape) for b in biases],
        out_specs=spec,
        out_shape=jax.ShapeDtypeStruct((BATCH, 8, N_RES, SLAB, LANES), bf16),
        scratch_shapes=[pltpu.VMEM((N_RES, SLAB, LANES), f32)] * 3,
        compiler_params=pltpu.CompilerParams(dimension_semantics=("arbitrary", "arbitrary"),
                                             vmem_limit_bytes=VMEM_LIMIT),
        name="dilated_attn",
    )(q, k, v, *biases)


def _post_kernel(h_ref, *rest, n_o, res_rows):
    o_refs = rest[:n_o]
    z_ref, p_ref, wo_ref, wg_ref, wp_ref, out_ref, mixed_sc, h1_sc = rest[n_o:]
    heads_per = D_MODEL // LANES // n_o
    for n, o_ref in enumerate(o_refs):
        for hh in range(heads_per):
            col = (n * heads_per + hh) * LANES
            if res_rows is None:
                o = o_ref[0, hh].astype(f32)
                z = z_ref[:, col:col + LANES].astype(f32)
                mixed_sc[:, col:col + LANES] = (o * (z * _sigmoid(z))).astype(bf16)
            else:
                for rr in range(8):
                    o = o_ref[0, hh, rr].astype(f32)
                    z = z_ref[0, rr, :, col:col + LANES].astype(f32)
                    mixed_sc[rr * res_rows:(rr + 1) * res_rows, col:col + LANES] = (
                        o * (z * _sigmoid(z))).astype(bf16)
    if res_rows is None:
        h, pb = h_ref[...], p_ref[...]
    else:
        h = jnp.concatenate([h_ref[0, :, rr, :] for rr in range(8)], axis=0)
        pb = jnp.concatenate([p_ref[0, :, rr, :] for rr in range(8)], axis=0)
    pb = pb.astype(bf16)
    h1 = h + jnp.dot(mixed_sc[...], wo_ref[...], preferred_element_type=f32)
    h1_sc[...] = h1
    h1b = h1.astype(bf16)
    for c in range(D_MODEL // CHUNK):
        cols = slice(c * CHUNK, (c + 1) * CHUNK)
        gate = _sigmoid(jnp.dot(h1b, wg_ref[:, cols], preferred_element_type=f32))
        emb = jnp.dot(pb, wp_ref[:, cols], preferred_element_type=f32)
        res = h1_sc[:, cols] + emb * gate
        if res_rows is None:
            out_ref[:, cols] = res
        else:
            for rr in range(8):
                out_ref[0, :, rr, cols] = res[rr * res_rows:(rr + 1) * res_rows]


def _post_even(x2d, oa, ob, z, p0, wo, wg, wp, tm):
    n_t = (BATCH * SEQ) // tm
    per_b = SEQ // tm
    row_spec = lambda w: pl.BlockSpec((tm, w), lambda i: (i, 0))
    head_spec = pl.BlockSpec((1, 4, tm, LANES), lambda i: (i // per_b, 0, i % per_b, 0))
    return pl.pallas_call(
        functools.partial(_post_kernel, n_o=2, res_rows=None),
        grid=(n_t,),
        in_specs=[row_spec(D_MODEL), head_spec, head_spec, row_spec(D_MODEL),
                  pl.BlockSpec((None, tm, PLE_DIM), lambda i: (0, i, 0)),
                  _const_spec(wo.shape), _const_spec(wg.shape), _const_spec(wp.shape)],
        out_specs=row_spec(D_MODEL),
        out_shape=jax.ShapeDtypeStruct((BATCH * SEQ, D_MODEL), f32),
        scratch_shapes=[pltpu.VMEM((tm, D_MODEL), bf16), pltpu.VMEM((tm, D_MODEL), f32)],
        compiler_params=pltpu.CompilerParams(dimension_semantics=("arbitrary",),
                                             vmem_limit_bytes=VMEM_LIMIT),
        name="post_even",
    )(x2d, oa, ob, z, p0, wo, wg, wp)


def _post_odd(h_res, o, z, p_res, wo, wg, wp):
    tm = 8 * RES_ROWS
    res_row = lambda w: pl.BlockSpec((1, RES_ROWS, 8, w), lambda b, g, c: (b, c, g, 0))
    return pl.pallas_call(
        functools.partial(_post_kernel, n_o=1, res_rows=RES_ROWS),
        grid=(BATCH, N_RES // 8, SLAB // RES_ROWS),
        in_specs=[res_row(D_MODEL),
                  pl.BlockSpec((1, 8, 8, RES_ROWS, LANES), lambda b, g, c: (b, 0, g, c, 0)),
                  pl.BlockSpec((1, 8, RES_ROWS, D_MODEL), lambda b, g, c: (b, g, c, 0)),
                  pl.BlockSpec((None, 1, RES_ROWS, 8, PLE_DIM), lambda b, g, c: (1, b, c, g, 0)),
                  _const_spec(wo.shape), _const_spec(wg.shape), _const_spec(wp.shape)],
        out_specs=res_row(D_MODEL),
        out_shape=jax.ShapeDtypeStruct((BATCH, SLAB, N_RES, D_MODEL), f32),
        scratch_shapes=[pltpu.VMEM((tm, D_MODEL), bf16), pltpu.VMEM((tm, D_MODEL), f32)],
        compiler_params=pltpu.CompilerParams(dimension_semantics=("arbitrary",) * 3,
                                             vmem_limit_bytes=VMEM_LIMIT),
        name="post_odd",
    )(h_res, o, z, p_res, wo, wg, wp)


def _tile4(g):
    return jnp.tile(g.astype(f32), CHUNK // g.shape[0])


def kernel(x, p, positions, norm_g, w_in_even, b_forget, qn_a, kn_a, qn_b, kn_b, lam_q1, lam_k1,
           lam_q2, lam_k2, subln_g, w_out_even, w_in_odd, qn_c, kn_c, w_out_odd, w_ple, w_ple_gate):
    tm = TOKEN_TILE
    tq_diff = tq_fox = ATTN_TILE
    n_tok = BATCH * SEQ
    x2d = x.reshape(n_tok, D_MODEL)
    pos_f = positions.astype(f32)

    w0 = w_in_even[0]
    f_lo, f_hi = 3072, 3076
    w0t = w0.T
    w_main0 = jnp.concatenate([w0t[:f_lo], w0t[f_hi:]], axis=0).astype(bf16)
    w_f = jnp.pad(w0t[f_lo:f_hi], ((0, LANES - 4), (0, 0))).astype(bf16)
    b_f = jnp.pad(b_forget[0].astype(f32), (0, LANES - 4)).reshape(1, LANES)
    ones_row = jnp.ones((CHUNK,), f32)
    gains0 = jnp.stack([_tile4(qn_a[0]) * (64 ** -0.5 * LOG2E), _tile4(kn_a[0]), ones_row,
                        _tile4(qn_b[0]) * (HEAD_DIM ** -0.5 * LOG2E), _tile4(kn_b[0]), ones_row,
                        ones_row, ones_row])
    qa, ka, va, qb, kb, vb, z0, logf = _proj_even(
        x2d, pos_f.reshape(n_tok // tm, 1, tm), norm_g[0].reshape(1, D_MODEL).astype(f32), w_main0, gains0,
        w_f, b_f, tm)
    cum = _forget_cumsum(logf)

    lam_rows = jnp.pad(jnp.stack([lam_q1[0], lam_k1[0], lam_q2[0], lam_k2[0]]).astype(f32),
                       ((0, 4), (0, LANES - 64)))
    lam_init = 0.8 - 0.6 * math.exp(-0.3 * 0)
    oa = _layer0_attention(
        qa, ka, va, [lam_rows, subln_g[0].reshape(LANES, 1).astype(f32)],
        [_const_spec((8, LANES)), _const_spec((LANES, 1))],
        functools.partial(_diff_attn_kernel, tq=tq_diff, lam_init=lam_init),
        2, tq_diff, "diff_attn")
    ob = _layer0_attention(
        qb, kb, vb, [cum], [pl.BlockSpec((1, 1, SEQ, LANES), lambda b, h: (b, h, 0, 0))],
        functools.partial(_fox_attn_kernel, tq=tq_fox), 1, tq_fox, "fox_attn")
    h1 = _post_even(x2d, oa, ob, z0, p.reshape(2, n_tok, PLE_DIM), w_out_even[0].astype(bf16),
                    w_ple_gate[0].astype(bf16), w_ple[0].astype(bf16), tm)

    h_res = h1.reshape(BATCH, SLAB, N_RES, D_MODEL)
    pos_res = (pos_f.reshape(BATCH, SLAB // RES_ROWS, RES_ROWS, N_RES // 8, 8)
               .transpose(0, 3, 1, 4, 2).reshape(-1, 1, 8 * RES_ROWS))
    ones_row2 = jnp.ones((CHUNK,), f32)
    gq = _tile4(qn_c[0]) * (HEAD_DIM ** -0.5 * LOG2E)
    gk = _tile4(kn_c[0])
    gains1 = jnp.stack([gq, gq, gk, gk, ones_row2, ones_row2, ones_row2, ones_row2])
    q1, k1, v1, z1 = _proj_odd(h_res, pos_res, norm_g[1].reshape(1, D_MODEL).astype(f32),
                               w_in_odd[0].astype(bf16), gains1)
    o1 = _dilated_attention(q1, k1, v1)
    out = _post_odd(h_res, o1, z1, p.reshape(2, BATCH, SLAB, N_RES, PLE_DIM),
                    w_out_odd[0].astype(bf16), w_ple_gate[1].astype(bf16), w_ple[1].astype(bf16))
    return out.reshape(BATCH, SEQ, D_MODEL)
```

```python
import functools
import math

import numpy as np
import jax
import jax.numpy as jnp
from jax import lax
from jax.experimental import pallas as pl
from jax.experimental.pallas import tpu as pltpu

D_MODEL = 1024
BATCH = 8
SEQ = 4096
PLE_DIM = 256
HEAD_DIM = 128
ROPE_THETA = 500000.0
RMS_EPS = 1e-6
LANES = 128
CHUNK = 512
N_RES = 16
SLAB = SEQ // N_RES
RES_ROWS = 64
VMEM_LIMIT = 56 * 1024 * 1024
NEG = -1e30
LOG2E = math.log2(math.e)
FLASH_GROUP = 8
SUM_ROWS = 16
UNROLL_16, UNROLL_4, UNROLL_1 = 16, 8, 16
TOKEN_TILE = 512
ATTN_TILE = 512

bf16 = jnp.bfloat16
f32 = jnp.float32


def _nt_dot(a, b):
    return lax.dot_general(a, b, (((1,), (1,)), ((), ())), preferred_element_type=f32)


def _sigmoid(x):
    return 1.0 / (1.0 + jnp.exp(-x))


def _proj_kernel(x_ref, pos_ref, ng_ref, w_ref, gains_ref, ones_ref, invf_ref,
                 *rest, chunks, rope_half, rope_group, has_f, res_rows, w_rows):
    if has_f:
        wf_ref, bf_ref = rest[0], rest[1]
        out_refs = rest[2:]
    else:
        out_refs = rest
    if res_rows is None:
        x = x_ref[...]
    else:
        x = jnp.concatenate([x_ref[0, :, rr, :] for rr in range(8)], axis=0)
    ms = jnp.mean(x * x, axis=-1, keepdims=True)
    hn = (x * lax.rsqrt(ms + RMS_EPS) * ng_ref[...]).astype(bf16)

    tm = x.shape[0]
    ang = invf_ref[...] * pos_ref[...].reshape(1, tm)
    cos_c, sin_c = jnp.cos(ang), jnp.sin(ang)
    zeros_c = jnp.zeros_like(ang)
    rest_rows = rope_group - 2 * rope_half
    reps = LANES // rope_group

    def lane_table(first, second, fill):
        blk = [first, second, jnp.full((rest_rows, tm), fill, f32)]
        return jnp.concatenate(blk * reps, axis=0).T

    cosv = lane_table(cos_c, cos_c, 1.0)
    sin_hi = lane_table(zeros_c, sin_c, 0.0)
    sin_lo = lane_table(-sin_c, zeros_c, 0.0)

    def put_head(o_ref, head, yb):
        if res_rows is None:
            o_ref[0, head] = yb
        else:
            for rr in range(8):
                o_ref[0, head, rr] = yb[rr * res_rows:(rr + 1) * res_rows]

    def put_cols(o_ref, col, yb):
        if res_rows is None:
            o_ref[:, col:col + LANES] = yb
        else:
            for rr in range(8):
                o_ref[0, rr, :, col:col + LANES] = yb[rr * res_rows:(rr + 1) * res_rows]

    for c, (kind, group, oi, slot) in enumerate(chunks):
        if w_rows:
            acc = _nt_dot(hn, w_ref[c * CHUNK:(c + 1) * CHUNK, :])
        else:
            acc = jnp.dot(hn, w_ref[:, c * CHUNK:(c + 1) * CHUNK], preferred_element_type=f32)
        if kind in ("rope", "norm") and group < LANES:
            ss = jnp.dot((acc * acc).astype(bf16), ones_ref[...], preferred_element_type=f32)
            acc = acc * lax.rsqrt(ss * (1.0 / group) + RMS_EPS) * gains_ref[c:c + 1, :]
        o_ref = out_refs[oi]
        for hh in range(CHUNK // LANES):
            yb = acc[:, hh * LANES:(hh + 1) * LANES]
            if kind in ("rope", "norm") and group == LANES:
                ms = jnp.mean(yb * yb, axis=-1, keepdims=True)
                yb = yb * lax.rsqrt(ms + RMS_EPS) * gains_ref[c:c + 1, hh * LANES:(hh + 1) * LANES]
            if kind == "rope":
                yb = (yb * cosv + pltpu.roll(yb, rope_half, 1) * sin_hi
                      + pltpu.roll(yb, LANES - rope_half, 1) * sin_lo)
            if kind == "plain_t":
                o_ref[0, slot * (CHUNK // LANES) + hh] = yb.T.astype(bf16)
            elif kind == "z":
                put_cols(o_ref, slot * CHUNK + hh * LANES, yb.astype(bf16))
            else:
                put_head(o_ref, slot * (CHUNK // LANES) + hh, yb.astype(bf16))

    if has_f:
        f = _nt_dot(hn, wf_ref[...]) + bf_ref[...]
        out_refs[-1][...] = jnp.minimum(f, 0.0) - jnp.log1p(jnp.exp(-jnp.abs(f)))


def _inv_freq_col(half):
    return jnp.exp(-math.log(ROPE_THETA) * jnp.arange(half, dtype=f32) / half).reshape(half, 1)


def _block_ones(group):
    i = np.arange(CHUNK)
    return jnp.asarray((i[:, None] // group == i[None, :] // group).astype(np.float32), dtype=bf16)


def _const_spec(shape):
    nd = len(shape)
    return pl.BlockSpec(shape, lambda *_: (0,) * nd)


def _proj_even(x2d, pos_col, ng, w_main, gains, w_f, b_f, tm):
    n_t = (BATCH * SEQ) // tm
    per_b = SEQ // tm
    chunks = (("rope", 64, 0, 0), ("rope", 64, 1, 0), ("plain_t", 0, 2, 0),
              ("norm", 128, 3, 0), ("norm", 128, 4, 0), ("plain_t", 0, 5, 0),
              ("z", 0, 6, 0), ("z", 0, 6, 1))
    head_shape = jax.ShapeDtypeStruct((BATCH, 4, SEQ, LANES), bf16)
    head_spec = pl.BlockSpec((1, 4, tm, LANES), lambda i: (i // per_b, 0, i % per_b, 0))
    head_t_shape = jax.ShapeDtypeStruct((BATCH, 4, LANES, SEQ), bf16)
    head_t_spec = pl.BlockSpec((1, 4, LANES, tm), lambda i: (i // per_b, 0, 0, i % per_b))
    row_spec = lambda w: pl.BlockSpec((tm, w), lambda i: (i, 0))
    kern = functools.partial(_proj_kernel, chunks=chunks, rope_half=8, rope_group=64, has_f=True,
                             res_rows=None, w_rows=True)
    return pl.pallas_call(
        kern,
        grid=(n_t,),
        in_specs=[row_spec(D_MODEL), pl.BlockSpec((1, 1, tm), lambda i: (i, 0, 0)),
                  _const_spec((1, D_MODEL)),
                  _const_spec(w_main.shape), _const_spec((8, CHUNK)),
                  _const_spec((CHUNK, CHUNK)),
                  _const_spec((8, 1)), _const_spec((LANES, D_MODEL)), _const_spec((1, LANES))],
        out_specs=[head_spec, head_spec, head_t_spec] * 2 + [row_spec(D_MODEL), row_spec(LANES)],
        out_shape=[head_shape, head_shape, head_t_shape] * 2 + [
            jax.ShapeDtypeStruct((BATCH * SEQ, D_MODEL), bf16),
            jax.ShapeDtypeStruct((BATCH * SEQ, LANES), f32)],
        compiler_params=pltpu.CompilerParams(dimension_semantics=("arbitrary",),
                                             vmem_limit_bytes=VMEM_LIMIT),
        name="proj_even",
    )(x2d, pos_col, ng, w_main, gains, _block_ones(64), _inv_freq_col(8),
      w_f, b_f)


def _proj_odd(h_res, pos_res, ng, w_main, gains):
    chunks = (("rope", 128, 0, 0), ("rope", 128, 0, 1), ("rope", 128, 1, 0), ("rope", 128, 1, 1),
              ("plain", 0, 2, 0), ("plain", 0, 2, 1), ("z", 0, 3, 0), ("z", 0, 3, 1))
    head_shape = jax.ShapeDtypeStruct((BATCH, 8, N_RES, SLAB, LANES), bf16)
    head_spec = pl.BlockSpec((1, 8, 8, RES_ROWS, LANES), lambda b, g, c: (b, 0, g, c, 0))
    kern = functools.partial(_proj_kernel, chunks=chunks, rope_half=16, rope_group=128, has_f=False,
                             res_rows=RES_ROWS, w_rows=False)
    n_g, n_c = N_RES // 8, SLAB // RES_ROWS
    return pl.pallas_call(
        kern,
        grid=(BATCH, n_g, n_c),
        in_specs=[pl.BlockSpec((1, RES_ROWS, 8, D_MODEL), lambda b, g, c: (b, c, g, 0)),
                  pl.BlockSpec((1, 1, 8 * RES_ROWS), lambda b, g, c: ((b * n_g + g) * n_c + c, 0, 0)),
                  _const_spec((1, D_MODEL)), _const_spec(w_main.shape), _const_spec((8, CHUNK)),
                  _const_spec((CHUNK, CHUNK)),
                  _const_spec((16, 1))],
        out_specs=[head_spec] * 3 + [pl.BlockSpec((1, 8, RES_ROWS, D_MODEL), lambda b, g, c: (b, g, c, 0))],
        out_shape=[head_shape] * 3 + [jax.ShapeDtypeStruct((BATCH, N_RES, SLAB, D_MODEL), bf16)],
        compiler_params=pltpu.CompilerParams(dimension_semantics=("arbitrary",) * 3,
                                             vmem_limit_bytes=VMEM_LIMIT),
        name="proj_odd",
    )(h_res, pos_res, ng, w_main, gains, _block_ones(64), _inv_freq_col(16))


def _split3(x):
    hi = x.astype(bf16).astype(f32)
    r1 = x - hi
    mid = r1.astype(bf16).astype(f32)
    lo = (r1 - mid).astype(bf16).astype(f32)
    return hi, mid, lo


def _split3_lanes(x):
    return jnp.concatenate(_split3(x), axis=1).astype(bf16)


def _cumsum_kernel(logf_ref, place_ref, aug_ref, *, blk):
    row = lax.broadcasted_iota(jnp.int32, (blk, blk), 0)
    col = lax.broadcasted_iota(jnp.int32, (blk, blk), 1)
    lower = (col <= row).astype(bf16)
    carry = jnp.zeros((1, LANES), f32)
    for i in range(SEQ // blk):
        x = logf_ref[i * blk:(i + 1) * blk, :]
        cs3 = jnp.dot(lower, _split3_lanes(x), preferred_element_type=f32)
        cs = (cs3[:, 0:LANES] + cs3[:, LANES:2 * LANES]) + cs3[:, 2 * LANES:3 * LANES] + carry
        carry = cs[blk - 1:blk, :]
        placed = jnp.dot(_split3_lanes(cs * (-LOG2E)), place_ref[...], preferred_element_type=f32)
        for hh in range(4):
            aug_ref[0, hh, i * blk:(i + 1) * blk, :] = placed[:, hh * LANES:(hh + 1) * LANES].astype(bf16)


def _place_matrices():
    pm = np.zeros((3 * LANES, 4 * LANES), np.float32)
    for k in range(3):
        for hh in range(4):
            pm[k * LANES + hh, hh * LANES + k] = 1.0
    return jnp.asarray(pm, dtype=bf16)


def _forget_cumsum(logf):
    return pl.pallas_call(
        functools.partial(_cumsum_kernel, blk=512),
        grid=(BATCH,),
        in_specs=[pl.BlockSpec((SEQ, LANES), lambda b: (b, 0)), _const_spec((3 * LANES, 4 * LANES))],
        out_specs=pl.BlockSpec((1, 4, SEQ, LANES), lambda b: (b, 0, 0, 0)),
        out_shape=jax.ShapeDtypeStruct((BATCH, 4, SEQ, LANES), bf16),
        compiler_params=pltpu.CompilerParams(dimension_semantics=("arbitrary",),
                                             vmem_limit_bytes=VMEM_LIMIT),
        name="forget_cumsum",
    )(logf, _place_matrices())


def _flash_sweep(qq, k_ref, vt_ref, aug_ref, acc_sc, m_sc, i, t):
    n = qq.shape[0]
    m_sc[...] = jnp.full(m_sc.shape, -jnp.inf, f32)
    acc_sc[...] = jnp.zeros(acc_sc.shape, f32)

    def scores(j):
        k = k_ref[0, 0, pl.ds(j * t, t), :]
        if aug_ref is not None:
            k = jnp.concatenate([k, aug_ref[0, 0, pl.ds(j * t, t), :]], axis=1)
        return _nt_dot(k, qq)

    def update(s, j, diag):
        if diag:
            key = lax.broadcasted_iota(jnp.int32, s.shape, 0)
            qry = lax.broadcasted_iota(jnp.int32, s.shape, 1)
            if n > t:
                qry = jnp.where(qry >= t, qry - t, qry)
            s = jnp.where(key <= qry, s, -jnp.inf)
        vt = vt_ref[0, 0, :, pl.ds(j * t, t)]
        vt_ext = jnp.concatenate([vt, jnp.ones((SUM_ROWS, t), bf16)], axis=0)
        m_prev = m_sc[...]
        m_new = jnp.maximum(m_prev, jnp.max(s, axis=0, keepdims=True))
        alpha = jnp.exp2(m_prev - m_new)
        p = jnp.exp2(s - m_new).astype(bf16)
        acc_sc[...] = alpha * acc_sc[...] + jnp.dot(vt_ext, p, preferred_element_type=f32)
        m_sc[...] = m_new

    def group(j0, size, last_diag):
        ss = [scores(j0 + d) for d in range(size)]
        for d in range(size):
            update(ss[d], j0 + d, last_diag and d == size - 1)

    for g in range(i // FLASH_GROUP):
        group(FLASH_GROUP * g, FLASH_GROUP, False)
    rem = i % FLASH_GROUP
    group(i - rem, rem + 1, True)


def _diff_attn_kernel(q_ref, k_ref, vt_ref, lam_ref, g_ref, o_ref, acc_sc, m_sc, *,
                      tq, lam_init):
    lam = (jnp.exp(jnp.sum(lam_ref[0:1, :] * lam_ref[1:2, :], axis=1, keepdims=True))
           - jnp.exp(jnp.sum(lam_ref[2:3, :] * lam_ref[3:4, :], axis=1, keepdims=True)) + lam_init)
    lane = lax.broadcasted_iota(jnp.int32, (tq, LANES), 1)

    for i in range(SEQ // tq):
        q = q_ref[0, 0, i * tq:(i + 1) * tq, :]
        zero = jnp.zeros_like(q)
        qq = jnp.concatenate([jnp.where(lane < 64, q, zero), jnp.where(lane >= 64, q, zero)], axis=0)
        _flash_sweep(qq, k_ref, vt_ref, None, acc_sc.at[i], m_sc.at[i], i, tq)
        ot = acc_sc[i, 0:LANES, :] / acc_sc[i, LANES:LANES + 1, :]
        ot = ot[:, 0:tq] - lam * ot[:, tq:2 * tq]
        ms = jnp.mean(ot * ot, axis=0, keepdims=True)
        ot = ot * lax.rsqrt(ms + RMS_EPS) * g_ref[...] * (1.0 - lam_init)
        o_ref[0, 0, i * tq:(i + 1) * tq, :] = ot.T.astype(bf16)


def _fox_attn_kernel(q_ref, k_ref, vt_ref, aug_ref, o_ref, acc_sc, m_sc, *, tq):
    lane = lax.broadcasted_iota(jnp.int32, (tq, LANES), 1)
    ones3 = jnp.where(lane < 3, 1.0, 0.0).astype(bf16)

    for i in range(SEQ // tq):
        qq = jnp.concatenate([q_ref[0, 0, i * tq:(i + 1) * tq, :], ones3], axis=1)
        _flash_sweep(qq, k_ref, vt_ref, aug_ref, acc_sc.at[i], m_sc.at[i], i, tq)
        o_ref[0, 0, i * tq:(i + 1) * tq, :] = (
            acc_sc[i, 0:LANES, :] / acc_sc[i, LANES:LANES + 1, :]).T.astype(bf16)


def _layer0_attention(q, k, vt, extra, extra_specs, kern, n_stack, tq, name):
    head_spec = pl.BlockSpec((1, 1, SEQ, LANES), lambda b, h: (b, h, 0, 0))
    vt_spec = pl.BlockSpec((1, 1, LANES, SEQ), lambda b, h: (b, h, 0, 0))
    n = n_stack * tq
    return pl.pallas_call(
        kern,
        grid=(BATCH, 4),
        in_specs=[head_spec, head_spec, vt_spec] + extra_specs,
        out_specs=head_spec,
        out_shape=jax.ShapeDtypeStruct((BATCH, 4, SEQ, LANES), bf16),
        scratch_shapes=[pltpu.VMEM((SEQ // tq, LANES + SUM_ROWS, n), f32),
                        pltpu.VMEM((SEQ // tq, 1, n), f32)],
        compiler_params=pltpu.CompilerParams(
            dimension_semantics=("arbitrary", "arbitrary"), vmem_limit_bytes=VMEM_LIMIT),
        name=name,
    )(q, k, vt, *extra)


def _pattern_bias(dil):
    n_c = N_RES // dil
    bq = max(128 // n_c, 16)
    bk = 2 * bq
    cq, iq = np.divmod(np.arange(n_c * bq), bq)
    ck, jk = np.divmod(np.arange(n_c * bk), bk)
    out = []
    for shift in (0, bq):
        dist = n_c * (shift + iq[:, None] - jk[None, :]) + (cq[:, None] - ck[None, :])
        out.append(np.where((dist >= 0) & (dist <= 128), 0.0, NEG))
    return jnp.asarray(np.stack(out), dtype=f32), n_c, bq


def _dilated_kernel(q_ref, k_ref, v_ref, b16_ref, b4_ref, b1_ref, o_ref, acc_sc, m_sc, l_sc, *, geo):
    def pieces(ref, lead, slabs, start, n):
        parts = [ref[lead + (s, pl.ds(start, n), slice(None))] for s in slabs]
        return parts[0] if len(parts) == 1 else jnp.concatenate(parts, axis=0)

    def load(slabs, ub, bq, bias_ref, first):
        u0 = ub * bq
        ks = jnp.maximum(ub - 1, 0) * bq
        t = dict(slabs=slabs, u0=u0, bq=bq, first=first,
                 q=pieces(q_ref, (0, 0), slabs, u0, bq),
                 k=pieces(k_ref, (0, 0), slabs, ks, 2 * bq),
                 v=pieces(v_ref, (0, 0), slabs, ks, 2 * bq),
                 bias=bias_ref[jnp.minimum(ub, 1)])
        if not first:
            t["m"] = pieces(m_sc, (), slabs, u0, bq)
            t["l"] = pieces(l_sc, (), slabs, u0, bq)
            t["a"] = pieces(acc_sc, (), slabs, u0, bq)
        return t

    def compute(t):
        s = _nt_dot(t["q"], t["k"]) + t["bias"]
        m_cur = jnp.max(s, axis=1, keepdims=True)
        v_ext = jnp.concatenate([t["v"], jnp.ones(t["v"].shape, bf16)], axis=1)
        if t["first"]:
            m_new = jnp.broadcast_to(m_cur, (s.shape[0], LANES))
        else:
            m_new = jnp.maximum(t["m"], m_cur)
            alpha = jnp.exp2(t["m"] - m_new)
        p = jnp.exp2(s - jnp.concatenate([m_new] * (s.shape[1] // LANES), axis=1))
        ext = jnp.dot(p.astype(bf16), v_ext, preferred_element_type=f32)
        a_new, l_new = ext[:, :LANES], ext[:, LANES:]
        if not t["first"]:
            l_new = l_new + alpha * t["l"]
            a_new = a_new + alpha * t["a"]
        return m_new, l_new, a_new

    def store(t, res):
        bq, u0 = t["bq"], t["u0"]
        for n, s_id in enumerate(t["slabs"]):
            rows = slice(n * bq, (n + 1) * bq)
            for ref, val in zip((m_sc, l_sc, acc_sc), res):
                ref[s_id, pl.ds(u0, bq), :] = val[rows]

    def run(tiles):
        results = [compute(t) for t in tiles]
        for t, res in zip(tiles, results):
            store(t, res)

    first = True
    for (dil, n_c, bq), bias_ref in zip(geo, (b16_ref, b4_ref, b1_ref)):
        n_tiles = SLAB // bq
        if n_c == 1:
            def body(rg, carry, bq=bq, bias_ref=bias_ref, first=first, n_tiles=n_tiles):
                run([load([UNROLL_16 * rg + dr], ub, bq, bias_ref, first)
                     for dr in range(UNROLL_16) for ub in range(n_tiles)])
                return carry
            lax.fori_loop(0, N_RES // UNROLL_16, body, 0)
        elif n_c == N_RES:
            def body(pp, carry, bq=bq, bias_ref=bias_ref, first=first):
                run([load(list(range(N_RES)), UNROLL_1 * pp + du, bq, bias_ref, first)
                     for du in range(UNROLL_1)])
                return carry
            lax.fori_loop(0, n_tiles // UNROLL_1, body, 0)
        else:
            def body(up, carry, dil=dil, n_c=n_c, bq=bq, bias_ref=bias_ref, first=first):
                run([load([c * dil + res for c in range(n_c)], UNROLL_4 * up + du, bq, bias_ref, first)
                     for du in range(UNROLL_4) for res in range(dil)])
                return carry
            lax.fori_loop(0, n_tiles // UNROLL_4, body, 0)
        first = False

    for r in range(N_RES):
        o_ref[0, 0, r] = (acc_sc[r] / l_sc[r]).astype(bf16)


def _dilated_attention(q, k, v):
    biases, geo = [], []
    for dil in (16, 4, 1):
        b, n_c, bq = _pattern_bias(dil)
        biases.append(b)
        geo.append((dil, n_c, bq))
    spec = pl.BlockSpec((1, 1, N_RES, SLAB, LANES), lambda b, h: (b, h, 0, 0, 0))
    return pl.pallas_call(
        functools.partial(_dilated_kernel, geo=tuple(geo)),
        grid=(BATCH, 8),
        in_specs=[spec, spec, spec] + [_const_spec(b.shape) for b in biases],
        out_specs=spec,
        out_shape=jax.ShapeDtypeStruct((BATCH, 8, N_RES, SLAB, LANES), bf16),
        scratch_shapes=[pltpu.VMEM((N_RES, SLAB, LANES), f32)] * 3,
        compiler_params=pltpu.CompilerParams(dimension_semantics=("arbitrary", "arbitrary"),
                                             vmem_limit_bytes=VMEM_LIMIT),
        name="dilated_attn",
    )(q, k, v, *biases)


def _post_kernel(h_ref, *rest, n_o, res_rows):
    o_refs = rest[:n_o]
    z_ref, p_ref, wo_ref, wg_ref, wp_ref, out_ref, mixed_sc, h1_sc = rest[n_o:]
    heads_per = D_MODEL // LANES // n_o
    for n, o_ref in enumerate(o_refs):
        for hh in range(heads_per):
            col = (n * heads_per + hh) * LANES
            if res_rows is None:
                o = o_ref[0, hh].astype(f32)
                z = z_ref[:, col:col + LANES].astype(f32)
                mixed_sc[:, col:col + LANES] = (o * (z * _sigmoid(z))).astype(bf16)
            else:
                for rr in range(8):
                    o = o_ref[0, hh, rr].astype(f32)
                    z = z_ref[0, rr, :, col:col + LANES].astype(f32)
                    mixed_sc[rr * res_rows:(rr + 1) * res_rows, col:col + LANES] = (
                        o * (z * _sigmoid(z))).astype(bf16)
    if res_rows is None:
        h, pb = h_ref[...], p_ref[...]
    else:
        h = jnp.concatenate([h_ref[0, :, rr, :] for rr in range(8)], axis=0)
        pb = jnp.concatenate([p_ref[0, :, rr, :] for rr in range(8)], axis=0)
    pb = pb.astype(bf16)
    h1 = h + jnp.dot(mixed_sc[...], wo_ref[...], preferred_element_type=f32)
    h1_sc[...] = h1
    h1b = h1.astype(bf16)
    for c in range(D_MODEL // CHUNK):
        cols = slice(c * CHUNK, (c + 1) * CHUNK)
        gate = _sigmoid(jnp.dot(h1b, wg_ref[:, cols], preferred_element_type=f32))
        emb = jnp.dot(pb, wp_ref[:, cols], preferred_element_type=f32)
        res = h1_sc[:, cols] + emb * gate
        if res_rows is None:
            out_ref[:, cols] = res
        else:
            for rr in range(8):
                out_ref[0, :, rr, cols] = res[rr * res_rows:(rr + 1) * res_rows]


def _post_even(x2d, oa, ob, z, p0, wo, wg, wp, tm):
    n_t = (BATCH * SEQ) // tm
    per_b = SEQ // tm
    row_spec = lambda w: pl.BlockSpec((tm, w), lambda i: (i, 0))
    head_spec = pl.BlockSpec((1, 4, tm, LANES), lambda i: (i // per_b, 0, i % per_b, 0))
    return pl.pallas_call(
        functools.partial(_post_kernel, n_o=2, res_rows=None),
        grid=(n_t,),
        in_specs=[row_spec(D_MODEL), head_spec, head_spec, row_spec(D_MODEL),
                  pl.BlockSpec((None, tm, PLE_DIM), lambda i: (0, i, 0)),
                  _const_spec(wo.shape), _const_spec(wg.shape), _const_spec(wp.shape)],
        out_specs=row_spec(D_MODEL),
        out_shape=jax.ShapeDtypeStruct((BATCH * SEQ, D_MODEL), f32),
        scratch_shapes=[pltpu.VMEM((tm, D_MODEL), bf16), pltpu.VMEM((tm, D_MODEL), f32)],
        compiler_params=pltpu.CompilerParams(dimension_semantics=("arbitrary",),
                                             vmem_limit_bytes=VMEM_LIMIT),
        name="post_even",
    )(x2d, oa, ob, z, p0, wo, wg, wp)


def _post_odd(h_res, o, z, p_res, wo, wg, wp):
    tm = 8 * RES_ROWS
    res_row = lambda w: pl.BlockSpec((1, RES_ROWS, 8, w), lambda b, g, c: (b, c, g, 0))
    return pl.pallas_call(
        functools.partial(_post_kernel, n_o=1, res_rows=RES_ROWS),
        grid=(BATCH, N_RES // 8, SLAB // RES_ROWS),
        in_specs=[res_row(D_MODEL),
                  pl.BlockSpec((1, 8, 8, RES_ROWS, LANES), lambda b, g, c: (b, 0, g, c, 0)),
                  pl.BlockSpec((1, 8, RES_ROWS, D_MODEL), lambda b, g, c: (b, g, c, 0)),
                  pl.BlockSpec((None, 1, RES_ROWS, 8, PLE_DIM), lambda b, g, c: (1, b, c, g, 0)),
                  _const_spec(wo.shape), _const_spec(wg.shape), _const_spec(wp.shape)],
        out_specs=res_row(D_MODEL),
        out_shape=jax.ShapeDtypeStruct((BATCH, SLAB, N_RES, D_MODEL), f32),
        scratch_shapes=[pltpu.VMEM((tm, D_MODEL), bf16), pltpu.VMEM((tm, D_MODEL), f32)],
        compiler_params=pltpu.CompilerParams(dimension_semantics=("arbitrary",) * 3,
                                             vmem_limit_bytes=VMEM_LIMIT),
        name="post_odd",
    )(h_res, o, z, p_res, wo, wg, wp)


def _tile4(g):
    return jnp.tile(g.astype(f32), CHUNK // g.shape[0])


def kernel(x, p, positions, norm_g, w_in_even, b_forget, qn_a, kn_a, qn_b, kn_b, lam_q1, lam_k1,
           lam_q2, lam_k2, subln_g, w_out_even, w_in_odd, qn_c, kn_c, w_out_odd, w_ple, w_ple_gate):
    tm = TOKEN_TILE
    tq_diff = tq_fox = ATTN_TILE
    n_tok = BATCH * SEQ
    x2d = x.reshape(n_tok, D_MODEL)
    pos_f = positions.astype(f32)

    w0 = w_in_even[0]
    f_lo, f_hi = 3072, 3076
    w0t = w0.T
    w_main0 = jnp.concatenate([w0t[:f_lo], w0t[f_hi:]], axis=0).astype(bf16)
    w_f = jnp.pad(w0t[f_lo:f_hi], ((0, LANES - 4), (0, 0))).astype(bf16)
    b_f = jnp.pad(b_forget[0].astype(f32), (0, LANES - 4)).reshape(1, LANES)
    ones_row = jnp.ones((CHUNK,), f32)
    gains0 = jnp.stack([_tile4(qn_a[0]) * (64 ** -0.5 * LOG2E), _tile4(kn_a[0]), ones_row,
                        _tile4(qn_b[0]) * (HEAD_DIM ** -0.5 * LOG2E), _tile4(kn_b[0]), ones_row,
                        ones_row, ones_row])
    qa, ka, va, qb, kb, vb, z0, logf = _proj_even(
        x2d, pos_f.reshape(n_tok // tm, 1, tm), norm_g[0].reshape(1, D_MODEL).astype(f32), w_main0, gains0,
        w_f, b_f, tm)
    cum = _forget_cumsum(logf)

    lam_rows = jnp.pad(jnp.stack([lam_q1[0], lam_k1[0], lam_q2[0], lam_k2[0]]).astype(f32),
                       ((0, 4), (0, LANES - 64)))
    lam_init = 0.8 - 0.6 * math.exp(-0.3 * 0)
    oa = _layer0_attention(
        qa, ka, va, [lam_rows, subln_g[0].reshape(LANES, 1).astype(f32)],
        [_const_spec((8, LANES)), _const_spec((LANES, 1))],
        functools.partial(_diff_attn_kernel, tq=tq_diff, lam_init=lam_init),
        2, tq_diff, "diff_attn")
    ob = _layer0_attention(
        qb, kb, vb, [cum], [pl.BlockSpec((1, 1, SEQ, LANES), lambda b, h: (b, h, 0, 0))],
        functools.partial(_fox_attn_kernel, tq=tq_fox), 1, tq_fox, "fox_attn")
    h1 = _post_even(x2d, oa, ob, z0, p.reshape(2, n_tok, PLE_DIM), w_out_even[0].astype(bf16),
                    w_ple_gate[0].astype(bf16), w_ple[0].astype(bf16), tm)

    h_res = h1.reshape(BATCH, SLAB, N_RES, D_MODEL)
    pos_res = (pos_f.reshape(BATCH, SLAB // RES_ROWS, RES_ROWS, N_RES // 8, 8)
               .transpose(0, 3, 1, 4, 2).reshape(-1, 1, 8 * RES_ROWS))
    ones_row2 = jnp.ones((CHUNK,), f32)
    gq = _tile4(qn_c[0]) * (HEAD_DIM ** -0.5 * LOG2E)
    gk = _tile4(kn_c[0])
    gains1 = jnp.stack([gq, gq, gk, gk, ones_row2, ones_row2, ones_row2, ones_row2])
    q1, k1, v1, z1 = _proj_odd(h_res, pos_res, norm_g[1].reshape(1, D_MODEL).astype(f32),
                               w_in_odd[0].astype(bf16), gains1)
    o1 = _dilated_attention(q1, k1, v1)
    out = _post_odd(h_res, o1, z1, p.reshape(2, BATCH, SLAB, N_RES, PLE_DIM),
                    w_out_odd[0].astype(bf16), w_ple_gate[1].astype(bf16), w_ple[1].astype(bf16))
    return out.reshape(BATCH, SEQ, D_MODEL)
```
